```python
import math
import jax, jax.numpy as jnp
from jax import lax
import numpy as np

D_MODEL = 1024
BATCH = 8
SEQ = 4096
DEPTH = 4

CHUNK = 64
EPS = 1e-6
CONV_K = 4
MIX_WIDTH = 2 * D_MODEL
SSD_WIDTH = MIX_WIDTH // 2
SSD_HEAD_DIM = 64
SSD_HEADS = SSD_WIDTH // SSD_HEAD_DIM
SSD_GROUPS = 2
SSD_HEADS_PER_GROUP = SSD_HEADS // SSD_GROUPS
SSD_STATE = 128
SSD_CONV_DIM = SSD_WIDTH + 2 * SSD_GROUPS * SSD_STATE
ATTN_WIDTH = MIX_WIDTH // 4
ATTN_HEAD_DIM = 64
ATTN_HEADS = ATTN_WIDTH // ATTN_HEAD_DIM
ATTN_BAND_CHUNKS = 9
ATTN_REL_CLIP = 2 * CHUNK
DN_WIDTH = MIX_WIDTH // 4
DN_HEAD_DIM = 128
DN_HEADS = DN_WIDTH // DN_HEAD_DIM
DN_CONV_DIM = 3 * DN_WIDTH
IN_SPLITS = (SSD_WIDTH, SSD_CONV_DIM, SSD_HEADS,
             ATTN_WIDTH, ATTN_WIDTH, ATTN_WIDTH, ATTN_WIDTH,
             DN_CONV_DIM, DN_WIDTH, DN_HEADS, DN_HEADS)
IN_DIM = sum(IN_SPLITS)
IN_OFFSETS = tuple(int(s) for s in np.cumsum(IN_SPLITS)[:-1])

kernel_name = "hybrid_ssd_bandattn_gdn_trunk"


def rms_norm(x, g):
    xf = x.astype(jnp.float32)
    y = xf * lax.rsqrt(jnp.mean(xf * xf, axis=-1, keepdims=True) + EPS)
    return (y * g.astype(jnp.float32)).astype(x.dtype)


def l2_norm(x):
    return x * lax.rsqrt(jnp.sum(x * x, axis=-1, keepdims=True) + EPS)


def causal_conv(x, w, b=None):
    y = lax.conv_general_dilated(
        x, w.astype(x.dtype)[:, None, :], window_strides=(1,), padding=[(CONV_K - 1, 0)],
        dimension_numbers=("NWC", "WIO", "NWC"), feature_group_count=x.shape[-1])
    if b is not None:
        y = y + b.astype(x.dtype)
    return y


def ssd_scan(x, dt, a, b_mat, c_mat):
    bsz, seq = x.shape[:2]
    nc = seq // CHUNK
    x = x.reshape(bsz, nc, CHUNK, *x.shape[2:])
    dt = dt.reshape(bsz, nc, CHUNK, *dt.shape[2:])
    b_mat = b_mat.reshape(bsz, nc, CHUNK, *b_mat.shape[2:])
    c_mat = c_mat.reshape(bsz, nc, CHUNK, *c_mat.shape[2:])
    xdt = x * dt[..., None]
    a_cum = jnp.cumsum(jnp.moveaxis(dt * a, (1, 2), (3, 4)), axis=-1)
    causal = jnp.tril(jnp.ones((CHUNK, CHUNK), dtype=bool))
    l_mat = jnp.exp(jnp.where(causal, a_cum[..., :, None] - a_cum[..., None, :], -jnp.inf))
    y_diag = jnp.einsum("bclgn,bcsgn,bgrcls,bcsgrp->bclgrp", c_mat, b_mat, l_mat, xdt)
    decay_states = jnp.exp(a_cum[..., -1:] - a_cum)
    states = jnp.einsum("bcsgn,bgrcs,bcsgrp->bcgrpn", b_mat, decay_states, xdt)
    chunk_decay = jnp.exp(a_cum[..., -1])

    def step(h, inp):
        s_c, d_c = inp
        return h * d_c[..., None, None] + s_c, h

    h0 = jnp.zeros_like(states[:, 0])
    _, prev = lax.scan(step, h0, (jnp.moveaxis(states, 1, 0), jnp.moveaxis(chunk_decay, -1, 0)))
    prev = jnp.moveaxis(prev, 0, 1)
    y_off = jnp.einsum("bclgn,bcgrpn,bgrcl->bclgrp", c_mat, prev, jnp.exp(a_cum))
    return (y_diag + y_off).reshape(bsz, seq, *x.shape[3:])


def band_attention(q, k, v, q_gain, k_gain, rel_bias):
    bsz, seq = q.shape[:2]
    nc = seq // CHUNK
    q = rms_norm(q, q_gain) * (ATTN_HEAD_DIM ** -0.5)
    k = rms_norm(k, k_gain)
    shp = (bsz, nc, CHUNK, ATTN_HEADS, ATTN_HEAD_DIM)
    qc, kc, vc = q.reshape(shp), k.reshape(shp), v.reshape(shp)
    pad = ((0, 0), (ATTN_BAND_CHUNKS - 1, 0), (0, 0), (0, 0), (0, 0))
    kp, vp = jnp.pad(kc, pad), jnp.pad(vc, pad)
    k_band = jnp.concatenate([kp[:, j:j + nc] for j in range(ATTN_BAND_CHUNKS)], axis=2)
    v_band = jnp.concatenate([vp[:, j:j + nc] for j in range(ATTN_BAND_CHUNKS)], axis=2)
    scores = jnp.einsum("bnqhd,bnkhd->bnhqk", qc, k_band).astype(jnp.float32)
    band_len = ATTN_BAND_CHUNKS * CHUNK
    q_pos = (ATTN_BAND_CHUNKS - 1) * CHUNK + jnp.arange(CHUNK)
    k_pos = jnp.arange(band_len)
    rel = jnp.clip(q_pos[:, None] - k_pos[None, :], -ATTN_REL_CLIP, ATTN_REL_CLIP) + ATTN_REL_CLIP
    bias = rel_bias[:, rel].astype(jnp.float32)
    chunk_idx = jnp.arange(nc)[:, None] - (ATTN_BAND_CHUNKS - 1) + k_pos[None, :] // CHUNK
    valid = chunk_idx >= 0
    scores = jnp.where(valid[None, :, None, None, :], scores + bias, -jnp.inf)
    probs = jax.nn.softmax(scores, axis=-1).astype(v.dtype)
    out = jnp.einsum("bnhqk,bnkhd->bnqhd", probs, v_band)
    return out.reshape(bsz, seq, ATTN_WIDTH)


def gated_delta_rule(q, k, v, g, beta):
    bsz, seq = q.shape[:2]
    nc = seq // CHUNK

    def to_chunks(t):
        return jnp.moveaxis(t.reshape(bsz, nc, CHUNK, *t.shape[2:]), 3, 1)

    q, k, v, g, beta = (to_chunks(t) for t in (q, k, v, g, beta))
    g_cum = jnp.cumsum(g, axis=-1)
    causal = jnp.tril(jnp.ones((CHUNK, CHUNK), dtype=bool))
    decay = jnp.exp(jnp.where(causal, g_cum[..., :, None] - g_cum[..., None, :], -jnp.inf))
    k_beta = k * beta[..., None]
    strict = jnp.tril(jnp.einsum("bhcid,bhcjd->bhcij", k_beta, k) * decay, -1)
    lhs = jnp.eye(CHUNK, dtype=q.dtype) + strict
    rhs = jnp.concatenate([v * beta[..., None], k_beta * jnp.exp(g_cum)[..., None]], axis=-1)
    sol = lax.linalg.triangular_solve(lhs, rhs, left_side=True, lower=True)
    u, w = sol[..., :DN_HEAD_DIM], sol[..., DN_HEAD_DIM:]
    attn = jnp.einsum("bhcid,bhcjd->bhcij", q, k) * decay
    q_dec = q * jnp.exp(g_cum)[..., None]
    g_last = g_cum[..., -1]
    k_dec = k * jnp.exp(g_last[..., None] - g_cum)[..., None]

    def step(state, inp):
        u_c, w_c, attn_c, q_c, k_c, d_c = inp
        v_new = u_c - jnp.einsum("bhld,bhde->bhle", w_c, state)
        out = jnp.einsum("bhld,bhde->bhle", q_c, state) + jnp.einsum("bhls,bhse->bhle", attn_c, v_new)
        state = state * d_c[..., None, None] + jnp.einsum("bhld,bhle->bhde", k_c, v_new)
        return state, out

    xs = tuple(jnp.moveaxis(t, 2, 0) for t in (u, w, attn, q_dec, k_dec, jnp.exp(g_last)))
    state0 = jnp.zeros((bsz, DN_HEADS, DN_HEAD_DIM, DN_HEAD_DIM), q.dtype)
    _, out = lax.scan(step, state0, xs)
    out = jnp.moveaxis(jnp.moveaxis(out, 0, 2), 1, 3)
    return out.reshape(bsz, seq, DN_HEADS, DN_HEAD_DIM)


def hybrid_layer(x, norm_g, w_in, ssd_conv_w, ssd_conv_b, ssd_dt_bias, ssd_a_log, ssd_d,
                 ssd_norm_g, attn_q_norm_g, attn_k_norm_g, attn_rel_bias, dn_conv_w,
                 dn_dt_bias, dn_a_log, dn_norm_g, w_out):
    f32 = jnp.float32
    bsz, seq, _ = x.shape
    h = rms_norm(x, norm_g)
    proj = h @ w_in
    (ssd_z, ssd_xbc, ssd_dt, a_q, a_k, a_v, a_gate,
     dn_qkv, dn_gate, dn_b, dn_a) = jnp.split(proj, IN_OFFSETS, axis=-1)

    xbc = jax.nn.silu(causal_conv(ssd_xbc, ssd_conv_w, ssd_conv_b)).astype(f32)
    xs, b_mat, c_mat = jnp.split(xbc, [SSD_WIDTH, SSD_WIDTH + SSD_GROUPS * SSD_STATE], axis=-1)
    xs = xs.reshape(bsz, seq, SSD_GROUPS, SSD_HEADS_PER_GROUP, SSD_HEAD_DIM)
    b_mat = b_mat.reshape(bsz, seq, SSD_GROUPS, SSD_STATE)
    c_mat = c_mat.reshape(bsz, seq, SSD_GROUPS, SSD_STATE)
    dt = jax.nn.softplus(ssd_dt.astype(f32) + ssd_dt_bias.astype(f32))
    dt = dt.reshape(bsz, seq, SSD_GROUPS, SSD_HEADS_PER_GROUP)
    a = -jnp.exp(ssd_a_log.astype(f32)).reshape(SSD_GROUPS, SSD_HEADS_PER_GROUP)
    d_skip = ssd_d.astype(f32).reshape(SSD_GROUPS, SSD_HEADS_PER_GROUP)[..., None]
    y = ssd_scan(xs, dt, a, b_mat, c_mat) + xs * d_skip
    y = y.reshape(bsz, seq, SSD_WIDTH) * jax.nn.silu(ssd_z.astype(f32))
    y_ssd = rms_norm(y.reshape(bsz, seq, SSD_GROUPS, SSD_WIDTH // SSD_GROUPS),
                     ssd_norm_g.reshape(SSD_GROUPS, SSD_WIDTH // SSD_GROUPS)).reshape(bsz, seq, SSD_WIDTH)

    hs = (bsz, seq, ATTN_HEADS, ATTN_HEAD_DIM)
    y_attn = band_attention(a_q.reshape(hs), a_k.reshape(hs), a_v.reshape(hs),
                            attn_q_norm_g, attn_k_norm_g, attn_rel_bias)
    y_attn = y_attn.astype(f32) * jax.nn.silu(a_gate.astype(f32))

    qkv = jax.nn.silu(causal_conv(dn_qkv, dn_conv_w)).astype(f32)
    dq, dk, dv = (t.reshape(bsz, seq, DN_HEADS, DN_HEAD_DIM) for t in jnp.split(qkv, 3, axis=-1))
    dq = l2_norm(dq) * (DN_HEAD_DIM ** -0.5)
    dk = l2_norm(dk)
    beta = jax.nn.sigmoid(dn_b.astype(f32))
    g = -jnp.exp(dn_a_log.astype(f32)) * jax.nn.softplus(dn_a.astype(f32) + dn_dt_bias.astype(f32))
    o = gated_delta_rule(dq, dk, dv, g, beta)
    y_dn = rms_norm(o, dn_norm_g).reshape(bsz, seq, DN_WIDTH) * jax.nn.silu(dn_gate.astype(f32))

    mixed = jnp.concatenate([y_ssd, y_attn, y_dn], axis=-1).astype(x.dtype)
    return x + mixed @ w_out


def setup_inputs(seed: int = 0) -> dict:
    key = jax.random.key(seed)
    ks = jax.random.split(key, 20)
    f32 = jnp.float32

    def gain(k, shape):
        return 1.0 + 0.01 * jax.random.normal(k, shape, f32)

    def dt_bias_init(k, shape):
        dt = jnp.exp(jax.random.uniform(k, shape, f32, math.log(1e-3), math.log(1e-1)))
        return dt + jnp.log(-jnp.expm1(-dt))

    def a_log_init(k, shape):
        return jnp.log(jax.random.uniform(k, shape, f32, 1.0, 16.0))

    L = DEPTH
    return {
        "x": jax.random.normal(ks[0], (BATCH, SEQ, D_MODEL), f32),
        "norm_g": gain(ks[1], (L, D_MODEL)),
        "w_in": jax.random.normal(ks[2], (L, D_MODEL, IN_DIM), f32) * D_MODEL ** -0.5,
        "ssd_conv_w": jax.random.normal(ks[3], (L, CONV_K, SSD_CONV_DIM), f32) * CONV_K ** -0.5,
        "ssd_conv_b": 0.01 * jax.random.normal(ks[4], (L, SSD_CONV_DIM), f32),
        "ssd_dt_bias": dt_bias_init(ks[5], (L, SSD_HEADS)),
        "ssd_a_log": a_log_init(ks[6], (L, SSD_HEADS)),
        "ssd_d": gain(ks[7], (L, SSD_HEADS)),
        "ssd_norm_g": gain(ks[8], (L, SSD_WIDTH)),
        "attn_q_norm_g": gain(ks[9], (L, ATTN_HEAD_DIM)),
        "attn_k_norm_g": gain(ks[10], (L, ATTN_HEAD_DIM)),
        "attn_rel_bias": 0.1 * jax.random.normal(ks[11], (L, ATTN_HEADS, 2 * ATTN_REL_CLIP + 1), f32),
        "dn_conv_w": jax.random.normal(ks[12], (L, CONV_K, DN_CONV_DIM), f32) * CONV_K ** -0.5,
        "dn_dt_bias": dt_bias_init(ks[13], (L, DN_HEADS)),
        "dn_a_log": a_log_init(ks[14], (L, DN_HEADS)),
        "dn_norm_g": gain(ks[15], (L, DN_HEAD_DIM)),
        "w_out": jax.random.normal(ks[16], (L, MIX_WIDTH, D_MODEL), f32) * (2.0 * MIX_WIDTH) ** -0.5,
    }


def reference(x, norm_g, w_in, ssd_conv_w, ssd_conv_b, ssd_dt_bias, ssd_a_log, ssd_d,
              ssd_norm_g, attn_q_norm_g, attn_k_norm_g, attn_rel_bias, dn_conv_w,
              dn_dt_bias, dn_a_log, dn_norm_g, w_out):
    for l in range(DEPTH):
        x = hybrid_layer(x, norm_g[l], w_in[l], ssd_conv_w[l], ssd_conv_b[l], ssd_dt_bias[l],
                         ssd_a_log[l], ssd_d[l], ssd_norm_g[l], attn_q_norm_g[l],
                         attn_k_norm_g[l], attn_rel_bias[l], dn_conv_w[l], dn_dt_bias[l],
                         dn_a_log[l], dn_norm_g[l], w_out[l])
    return x
```

```python
import functools

import numpy as np
import jax
import jax.numpy as jnp
from jax import lax
from jax.experimental import pallas as pl
from jax.experimental.pallas import tpu as pltpu

F32 = jnp.float32
BF16 = jnp.bfloat16

D_MODEL = 1024
CHUNK = 64
EPS = 1e-6
CONV_K = 4
SSD_WIDTH = 1024
SSD_HEADS = 16
SSD_HEAD_DIM = 64
SSD_GROUPS = 2
SSD_STATE = 128
SSD_CONV_DIM = 1536
ATTN_WIDTH = 512
ATTN_HEADS = 8
ATTN_HEAD_DIM = 64
ATTN_BAND_CHUNKS = 9
ATTN_REL_CLIP = 128
DN_WIDTH = 512
DN_HEADS = 4
DN_HEAD_DIM = 128
DN_CONV_DIM = 1536
MIX_WIDTH = 2048

MAIN_WIDTH = 6656
XBC_BLK = 0
DNQKV_BLK = 1
Z_BLK = 3
AQ_BLK, AK_BLK, AV_BLK, AGATE_BLK = 8, 9, 10, 11
DNGATE_BLK = 12
SMALL_WIDTH = 128
DNB_COL = 16
DNA_COL = 20

LANES = 128
NEG = -1e30
VMEM_LIMIT = 56 * 1024 * 1024

ROWS = 256
CPS = ROWS // CHUNK
ATTN_TQ = 128
ATTN_NK = ATTN_TQ + (ATTN_BAND_CHUNKS - 1) * CHUNK
ATTN_PAD = (ATTN_BAND_CHUNKS - 1) * CHUNK


def _dot(a, b):
    return jnp.dot(a, b, preferred_element_type=F32)


def _dot_nt(a, b):
    return lax.dot_general(a, b, (((1,), (1,)), ((), ())), preferred_element_type=F32)


def _split(v):
    hi = v.astype(BF16)
    lo = (v - hi.astype(F32)).astype(BF16)
    return hi, lo


def _dot2(v, m):
    hi, lo = _split(v)
    return _dot(hi, m) + _dot(lo, m)


def _dot2l(m, v):
    hi, lo = _split(v)
    return _dot(m, hi) + _dot(m, lo)


def _silu(x):
    return x * jax.nn.sigmoid(x)


def _softplus(x):
    return jnp.maximum(x, 0.0) + jnp.log1p(jnp.exp(-jnp.abs(x)))


def _conv_silu(x_ref, xe_ref, cw_ref, bias, first, rows):
    width = xe_ref.shape[1]

    @pl.when(first)
    def _():
        xe_ref[0:8, :] = jnp.zeros((8, width), F32)

    xe_ref[8:rows + 8, :] = x_ref[...].astype(F32)
    acc = cw_ref[0:1, :] * xe_ref[5:5 + rows, :]
    for t in range(1, CONV_K):
        acc = acc + cw_ref[t:t + 1, :] * xe_ref[5 + t:5 + t + rows, :]
    if bias is not None:
        acc = acc + bias
    xe_ref[0:8, :] = xe_ref[rows:rows + 8, :]
    return _silu(acc)


def _inproj_kernel(x_ref, g_ref, w_ref, ws_ref, main_ref, small_ref, h_ref):
    @pl.when(pl.program_id(1) == 0)
    def _():
        x = x_ref[...]
        ms = jnp.mean(x * x, axis=-1, keepdims=True)
        h = (x * lax.rsqrt(ms + EPS) * g_ref[...]).astype(BF16)
        h_ref[...] = h
        small_ref[...] = _dot(h, ws_ref[...])

    main_ref[...] = _dot(h_ref[...], w_ref[...]).astype(BF16)


def _inproj(x2, g, w_main, w_small):
    t = x2.shape[0]
    tm = min(1024, t)
    tn = MAIN_WIDTH // 2
    return pl.pallas_call(
        _inproj_kernel,
        grid=(t // tm, MAIN_WIDTH // tn),
        in_specs=[
            pl.BlockSpec((tm, D_MODEL), lambda i, j: (i, 0)),
            pl.BlockSpec((1, D_MODEL), lambda i, j: (0, 0)),
            pl.BlockSpec((D_MODEL, tn), lambda i, j: (0, j)),
            pl.BlockSpec((D_MODEL, SMALL_WIDTH), lambda i, j: (0, 0)),
        ],
        out_specs=[
            pl.BlockSpec((tm, tn), lambda i, j: (i, j)),
            pl.BlockSpec((tm, SMALL_WIDTH), lambda i, j: (i, 0)),
        ],
        out_shape=[
            jax.ShapeDtypeStruct((t, MAIN_WIDTH), BF16),
            jax.ShapeDtypeStruct((t, SMALL_WIDTH), F32),
        ],
        scratch_shapes=[pltpu.VMEM((tm, D_MODEL), BF16)],
        compiler_params=pltpu.CompilerParams(
            dimension_semantics=("arbitrary", "arbitrary"), vmem_limit_bytes=VMEM_LIMIT),
        name="inproj",
    )(x2, g, w_main, w_small)


def _outproj_kernel(x_ref, ys_ref, ya_ref, yd_ref, ws_ref, wa_ref, wd_ref, o_ref):
    acc = _dot(ys_ref[...], ws_ref[...])
    acc = acc + _dot(ya_ref[...], wa_ref[...])
    acc = acc + _dot(yd_ref[...], wd_ref[...])
    o_ref[...] = x_ref[...] + acc


def _outproj(x2, y_ssd, y_attn, y_dn, wo_s, wo_a, wo_d):
    t = x2.shape[0]
    tm = min(512, t)
    row = lambda w: pl.BlockSpec((tm, w), lambda i: (i, 0))
    full = lambda r: pl.BlockSpec((r, D_MODEL), lambda i: (0, 0))
    return pl.pallas_call(
        _outproj_kernel,
        grid=(t // tm,),
        in_specs=[row(D_MODEL), row(SSD_WIDTH), row(ATTN_WIDTH), row(DN_WIDTH),
                  full(SSD_WIDTH), full(ATTN_WIDTH), full(DN_WIDTH)],
        out_specs=row(D_MODEL),
        out_shape=jax.ShapeDtypeStruct((t, D_MODEL), F32),
        compiler_params=pltpu.CompilerParams(
            dimension_semantics=("arbitrary",), vmem_limit_bytes=VMEM_LIMIT),
        name="outproj",
    )(x2, y_ssd, y_attn, y_dn, wo_s, wo_a, wo_d)


def _ssd_kernel(xbc_ref, z_ref, sm_ref, cw_ref, cb_ref, dtb_ref, alog_ref, dexp_ref, ng_ref,
                e_ref, ltri_ref, u0_ref, u1_ref, y_ref, xe_ref, st_ref):
    first = pl.program_id(1) == 0

    @pl.when(first)
    def _():
        st_ref[...] = jnp.zeros_like(st_ref)

    xa = _conv_silu(xbc_ref, xe_ref, cw_ref, cb_ref[...], first, ROWS)
    xs = xa[:, 0:SSD_WIDTH]
    bm = xa[:, SSD_WIDTH:SSD_WIDTH + 256]
    cm = xa[:, SSD_WIDTH + 256:SSD_WIDTH + 512]

    dt = _softplus(sm_ref[...] + dtb_ref[...])
    d_a = dt * (-jnp.exp(alog_ref[...]))
    acum = _dot2l(ltri_ref[...], d_a)
    d_at = d_a.T[0:16, :]
    q0 = _dot2(d_at, u0_ref[...])
    q1 = _dot2(d_at, u1_ref[...])
    prow = q0[0:8, :] + q1[8:16, :]

    e = e_ref[...]
    acum_x = _dot2(acum, e)
    dt_x = _dot2(dt, e)
    xdt = xs * dt_x
    eacum = jnp.exp(acum_x)

    row_i = lax.broadcasted_iota(jnp.int32, (CHUNK, LANES), 0)
    lane = lax.broadcasted_iota(jnp.int32, (CHUNK, LANES), 1)
    lane_lo = lane < CHUNK
    tri2 = row_i >= jnp.where(lane_lo, lane, lane - CHUNK)

    ys = []
    for ci in range(CPS):
        rs = slice(ci * CHUNK, (ci + 1) * CHUNK)
        ax = acum_x[rs]
        last = ax[CHUNK - 1:CHUNK, :]
        dec = jnp.exp(last - ax)
        cdec = jnp.exp(last)
        xdt_c = xdt[rs]
        xdd = (xdt_c * dec).astype(BF16)
        xdt_b = xdt_c.astype(BF16)
        pc = prow[:, ci * LANES:(ci + 1) * LANES]
        ydiag, yoff = [], []
        for g in range(SSD_GROUPS):
            gs = slice(g * 512, (g + 1) * 512)
            bg = bm[rs, g * SSD_STATE:(g + 1) * SSD_STATE]
            cg = cm[rs, g * SSD_STATE:(g + 1) * SSD_STATE].astype(BF16)
            bgt2 = jnp.concatenate([bg, bg], axis=0).T.astype(BF16)
            cb2 = _dot(cg, bgt2)
            st_g = st_ref[:, gs]
            yoff.append(_dot(cg, st_g.astype(BF16)))
            new_g = _dot(bgt2[:, 0:CHUNK], xdd[:, gs])
            st_ref[:, gs] = st_g * cdec[:, gs] + new_g
            for jp in range(4):
                j = g * 4 + jp
                tl = slice(j * LANES, (j + 1) * LANES)
                diff = ax[:, tl] - pc[j:j + 1, :]
                lp = jnp.exp(jnp.where(tri2, diff, NEG))
                mp = cb2 * lp
                m_lo = jnp.where(lane_lo, mp, 0.0).astype(BF16)
                m_hi = jnp.where(lane_lo, 0.0, mp).astype(BF16)
                xt = xdt_b[:, tl]
                x2 = jnp.concatenate([xt, xt], axis=0)
                ydiag.append(jnp.where(lane_lo, _dot(m_lo, x2), _dot(m_hi, x2)))
        y_c = (jnp.concatenate(ydiag, axis=1) + jnp.concatenate(yoff, axis=1) * eacum[rs]
               + xs[rs] * dexp_ref[...])
        ys.append(y_c)
    y = jnp.concatenate(ys, axis=0) * _silu(z_ref[...].astype(F32))
    outs = []
    for g in range(SSD_GROUPS):
        gs = slice(g * 512, (g + 1) * 512)
        yg = y[:, gs]
        ms = jnp.mean(yg * yg, axis=-1, keepdims=True)
        outs.append(yg * lax.rsqrt(ms + EPS) * ng_ref[:, gs])
    y_ref[...] = jnp.concatenate(outs, axis=1).astype(BF16)


def _ssd(main, small, nb, ns, cw, cb, dtb, alog, dexp, ng, consts):
    t = main.shape[0]
    spb = ns // ROWS
    rowmap = lambda blk: (lambda b, c: (b * spb + c, blk))
    const = lambda shp: pl.BlockSpec(shp, lambda b, c: (0, 0))
    return pl.pallas_call(
        _ssd_kernel,
        grid=(nb, spb),
        in_specs=[
            pl.BlockSpec((ROWS, SSD_CONV_DIM), rowmap(XBC_BLK)),
            pl.BlockSpec((ROWS, SSD_WIDTH), rowmap(Z_BLK)),
            pl.BlockSpec((ROWS, SMALL_WIDTH), rowmap(0)),
            const((CONV_K, SSD_CONV_DIM)), const((1, SSD_CONV_DIM)),
            const((1, SMALL_WIDTH)), const((1, SMALL_WIDTH)),
            const((1, SSD_WIDTH)), const((1, SSD_WIDTH)),
            const((SMALL_WIDTH, SSD_WIDTH)), const((ROWS, ROWS)),
            const((ROWS, CPS * LANES)), const((ROWS, CPS * LANES)),
        ],
        out_specs=pl.BlockSpec((ROWS, SSD_WIDTH), rowmap(0)),
        out_shape=jax.ShapeDtypeStruct((t, SSD_WIDTH), BF16),
        scratch_shapes=[pltpu.VMEM((ROWS + 8, SSD_CONV_DIM), F32),
                        pltpu.VMEM((SSD_STATE, SSD_WIDTH), F32)],
        compiler_params=pltpu.CompilerParams(
            dimension_semantics=("arbitrary", "arbitrary"), vmem_limit_bytes=VMEM_LIMIT),
        name="ssd",
    )(main, main, small, cw, cb, dtb, alog, dexp, ng,
      consts["e_ssd"], consts["ltri"], consts["u0"], consts["u1"])


def _aprep_kernel(q_ref, k_ref, v_ref, gq_ref, gk_ref, bd_ref, qo_ref, ko_ref, vo_ref):
    bd = bd_ref[...]
    q = q_ref[...].astype(F32)
    qms = _dot((q * q).astype(BF16), bd)
    qo_ref[...] = (q * lax.rsqrt(qms + EPS) * gq_ref[...] * (ATTN_HEAD_DIM ** -0.5)).astype(BF16)
    pad = pl.program_id(1) == 0

    @pl.when(pad)
    def _():
        ko_ref[...] = jnp.zeros_like(ko_ref)
        vo_ref[...] = jnp.zeros_like(vo_ref)

    @pl.when(jnp.logical_not(pad))
    def _():
        k = k_ref[...].astype(F32)
        kms = _dot((k * k).astype(BF16), bd)
        ko_ref[...] = (k * lax.rsqrt(kms + EPS) * gk_ref[...]).astype(BF16)
        vo_ref[...] = v_ref[...]


def _aprep(main, nb, ns, gq, gk, bd64):
    t = main.shape[0]
    rb = ATTN_PAD
    spb = ns // rb
    src = lambda blk: (lambda b, i: (b * spb + jnp.maximum(i - 1, 0), blk))
    const = lambda shp: pl.BlockSpec(shp, lambda b, i: (0, 0))
    tp = nb * (ns + ATTN_PAD)
    return pl.pallas_call(
        _aprep_kernel,
        grid=(nb, spb + 1),
        in_specs=[
            pl.BlockSpec((rb, ATTN_WIDTH), src(AQ_BLK)),
            pl.BlockSpec((rb, ATTN_WIDTH), src(AK_BLK)),
            pl.BlockSpec((rb, ATTN_WIDTH), src(AV_BLK)),
            const((1, ATTN_WIDTH)), const((1, ATTN_WIDTH)), const((ATTN_WIDTH, ATTN_WIDTH)),
        ],
        out_specs=[
            pl.BlockSpec((rb, ATTN_WIDTH), lambda b, i: (b * spb + jnp.maximum(i - 1, 0), 0)),
            pl.BlockSpec((rb, ATTN_WIDTH), lambda b, i: (b * (spb + 1) + i, 0)),
            pl.BlockSpec((rb, ATTN_WIDTH), lambda b, i: (b * (spb + 1) + i, 0)),
        ],
        out_shape=[
            jax.ShapeDtypeStruct((t, ATTN_WIDTH), BF16),
            jax.ShapeDtypeStruct((tp, ATTN_WIDTH), BF16),
            jax.ShapeDtypeStruct((tp, ATTN_WIDTH), BF16),
        ],
        compiler_params=pltpu.CompilerParams(
            dimension_semantics=("arbitrary", "arbitrary"), vmem_limit_bytes=VMEM_LIMIT),
        name="attn_prep",
    )(main, main, main, gq, gk, bd64)


def _attn_kernel(q_ref, k_ref, v_ref, gate_ref, bias_ref, o_ref):
    qi = pl.program_id(1)
    start = pl.multiple_of(qi * ATTN_TQ, ATTN_TQ)
    kk = k_ref[pl.ds(start, ATTN_NK), :]
    vv = v_ref[pl.ds(start, ATTN_NK), :]
    q = q_ref[...]
    col = lax.broadcasted_iota(jnp.int32, (1, ATTN_NK), 1)
    padrow = jnp.where(col < ATTN_PAD - qi * ATTN_TQ, NEG, 0.0)
    lane_lo = lax.broadcasted_iota(jnp.int32, (ATTN_TQ, LANES), 1) < ATTN_HEAD_DIM
    outs = []
    for jp in range(ATTN_HEADS // 2):
        tl = slice(jp * LANES, (jp + 1) * LANES)
        qp, kp, vp = q[:, tl], kk[:, tl], vv[:, tl]
        o2 = []
        for s in range(2):
            keep = lane_lo if s == 0 else jnp.logical_not(lane_lo)
            qm = jnp.where(keep, qp, jnp.zeros_like(qp))
            sc = _dot_nt(qm, kp) + bias_ref[2 * jp + s] + padrow
            m = jnp.max(sc, axis=-1, keepdims=True)
            p = jnp.exp(sc - m)
            l = jnp.sum(p, axis=-1, keepdims=True)
            o2.append(_dot(p.astype(BF16), vp) / l)
        outs.append(jnp.where(lane_lo, o2[0], o2[1]))
    o = jnp.concatenate(outs, axis=1) * _silu(gate_ref[...].astype(F32))
    o_ref[...] = o.astype(BF16)


def _attn(qn, kn, vn, main, nb, ns, bias):
    t = qn.shape[0]
    spb = ns // ATTN_TQ
    kv_rows = ns + ATTN_PAD
    return pl.pallas_call(
        _attn_kernel,
        grid=(nb, spb),
        in_specs=[
            pl.BlockSpec((ATTN_TQ, ATTN_WIDTH), lambda b, i: (b * spb + i, 0)),
            pl.BlockSpec((kv_rows, ATTN_WIDTH), lambda b, i: (b, 0)),
            pl.BlockSpec((kv_rows, ATTN_WIDTH), lambda b, i: (b, 0)),
            pl.BlockSpec((ATTN_TQ, ATTN_WIDTH), lambda b, i: (b * spb + i, AGATE_BLK)),
            pl.BlockSpec((ATTN_HEADS, ATTN_TQ, ATTN_NK), lambda b, i: (0, 0, 0)),
        ],
        out_specs=pl.BlockSpec((ATTN_TQ, ATTN_WIDTH), lambda b, i: (b * spb + i, 0)),
        out_shape=jax.ShapeDtypeStruct((t, ATTN_WIDTH), BF16),
        compiler_params=pltpu.CompilerParams(
            dimension_semantics=("arbitrary", "arbitrary"), vmem_limit_bytes=VMEM_LIMIT),
        name="attn",
    )(qn, kn, vn, main, bias)


def _tri_inv_minus_eye(a, eye, blk):
    d = jnp.where(blk, a, 0.0)
    o = a - d
    db = d.astype(BF16)
    d2 = _dot(db, db)
    d2b = d2.astype(BF16)
    d4 = _dot(d2b, d2b)
    d4b = d4.astype(BF16)
    d8 = _dot(d4b, d4b)
    x = eye - d
    x = x + _dot(x.astype(BF16), d2b)
    x = x + _dot(x.astype(BF16), d4b)
    td = x + _dot(x.astype(BF16), d8.astype(BF16))
    tdb = td.astype(BF16)
    n = o + _dot((td - eye).astype(BF16), o.astype(BF16))
    nb = n.astype(BF16)
    n2 = _dot(nb, nb)
    tn_m = -n + _dot((eye - n).astype(BF16), n2.astype(BF16))
    return (td - eye) + tn_m + _dot(tn_m.astype(BF16), (td - eye).astype(BF16))


def _gdn_kernel(qkv_ref, gate_ref, sm_ref, cw_ref, dtb_ref, alog_ref, ng_ref, eb_ref, eg_ref,
                ltri_ref, u0_ref, bd_ref, o_ref, xe_ref, st_ref):
    first = pl.program_id(1) == 0

    @pl.when(first)
    def _():
        st_ref[...] = jnp.zeros_like(st_ref)

    qkv = _conv_silu(qkv_ref, xe_ref, cw_ref, None, first, ROWS)
    q = qkv[:, 0:DN_WIDTH]
    k = qkv[:, DN_WIDTH:2 * DN_WIDTH]
    v = qkv[:, 2 * DN_WIDTH:3 * DN_WIDTH]
    bd = bd_ref[...]
    q = q * lax.rsqrt(_dot((q * q).astype(BF16), bd) + EPS) * (DN_HEAD_DIM ** -0.5)
    k = k * lax.rsqrt(_dot((k * k).astype(BF16), bd) + EPS)

    sm = sm_ref[...]
    beta_s = jax.nn.sigmoid(sm)
    g_s = -jnp.exp(alog_ref[...]) * _softplus(sm + dtb_ref[...])
    gcum = _dot2l(ltri_ref[...], g_s)
    g_t = g_s.T[16:32, :]
    grow = _dot2(g_t, u0_ref[...])
    beta_x = _dot2(beta_s, eb_ref[...])
    gc_x = _dot2(gcum, eg_ref[...])
    egc = jnp.exp(gc_x)
    kb = k * beta_x
    vb = v * beta_x
    kbg = kb * egc
    qd = q * egc

    ri = lax.broadcasted_iota(jnp.int32, (CHUNK, CHUNK), 0)
    cj = lax.broadcasted_iota(jnp.int32, (CHUNK, CHUNK), 1)
    tri = ri >= cj
    stri = ri > cj
    blk = (ri >> 4) == (cj >> 4)
    eye = jnp.where(ri == cj, 1.0, 0.0).astype(F32)

    outs = []
    for ci in range(CPS):
        rs = slice(ci * CHUNK, (ci + 1) * CHUNK)
        gx = gc_x[rs]
        glast = gx[CHUNK - 1:CHUNK, :]
        kd = k[rs] * jnp.exp(glast - gx)
        sdec = jnp.exp(glast)
        heads = []
        for h in range(DN_HEADS):
            th = slice(h * DN_HEAD_DIM, (h + 1) * DN_HEAD_DIM)
            kh = k[rs, th].astype(BF16)
            lhs = jnp.concatenate([kb[rs, th], q[rs, th]], axis=0).astype(BF16)
            kq = _dot_nt(lhs, kh)
            g_row = grow[4 + h:5 + h, ci * LANES:ci * LANES + CHUNK]
            diff = gx[:, h * DN_HEAD_DIM:h * DN_HEAD_DIM + CHUNK] - g_row
            dm = jnp.exp(jnp.where(tri, diff, NEG))
            a = jnp.where(stri, kq[0:CHUNK] * dm, 0.0)
            attn = kq[CHUNK:2 * CHUNK] * dm
            tm = _tri_inv_minus_eye(a, eye, blk)
            rhs = jnp.concatenate([vb[rs, th], kbg[rs, th]], axis=1)
            sol = rhs + _dot(tm.astype(BF16), rhs.astype(BF16))
            u = sol[:, 0:DN_HEAD_DIM]
            w = sol[:, DN_HEAD_DIM:2 * DN_HEAD_DIM]
            s_h = st_ref[h]
            wq = _dot(jnp.concatenate([w, qd[rs, th]], axis=0).astype(BF16), s_h.astype(BF16))
            vn = u - wq[0:CHUNK]
            vnb = vn.astype(BF16)
            heads.append(wq[CHUNK:2 * CHUNK] + _dot(attn.astype(BF16), vnb))
            st_ref[h] = s_h * sdec[:, th] + _dot(kd[:, th].T.astype(BF16), vnb)
        outs.append(jnp.concatenate(heads, axis=1))
    o = jnp.concatenate(outs, axis=0)
    ms = _dot((o * o).astype(BF16), bd) * (1.0 / DN_HEAD_DIM)
    o = o * lax.rsqrt(ms + EPS) * ng_ref[...]
    o_ref[...] = (o * _silu(gate_ref[...].astype(F32))).astype(BF16)


def _gdn(main, small, nb, ns, cw, dtb, alog, ng, consts):
    t = main.shape[0]
    spb = ns // ROWS
    rowmap = lambda blk: (lambda b, c: (b * spb + c, blk))
    const = lambda shp: pl.BlockSpec(shp, lambda b, c: (0, 0))
    return pl.pallas_call(
        _gdn_kernel,
        grid=(nb, spb),
        in_specs=[
            pl.BlockSpec((ROWS, DN_CONV_DIM), rowmap(DNQKV_BLK)),
            pl.BlockSpec((ROWS, DN_WIDTH), rowmap(DNGATE_BLK)),
            pl.BlockSpec((ROWS, SMALL_WIDTH), rowmap(0)),
            const((CONV_K, DN_CONV_DIM)),
            const((1, SMALL_WIDTH)), const((1, SMALL_WIDTH)), const((1, DN_WIDTH)),
            const((SMALL_WIDTH, DN_WIDTH)), const((SMALL_WIDTH, DN_WIDTH)),
            const((ROWS, ROWS)), const((ROWS, CPS * LANES)), const((DN_WIDTH, DN_WIDTH)),
        ],
        out_specs=pl.BlockSpec((ROWS, DN_WIDTH), rowmap(0)),
        out_shape=jax.ShapeDtypeStruct((t, DN_WIDTH), BF16),
        scratch_shapes=[pltpu.VMEM((ROWS + 8, DN_CONV_DIM), F32),
                        pltpu.VMEM((DN_HEADS, DN_HEAD_DIM, DN_HEAD_DIM), F32)],
        compiler_params=pltpu.CompilerParams(
            dimension_semantics=("arbitrary", "arbitrary"), vmem_limit_bytes=VMEM_LIMIT),
        name="gdn",
    )(main, main, small, cw, dtb, alog, ng,
      consts["e_beta"], consts["e_g"], consts["ltri"], consts["u0"], consts["bd128"])


_HEAD_PERM = np.concatenate([np.arange(0, SSD_HEADS, 2), np.arange(1, SSD_HEADS, 2)])


def _constants():
    r = np.arange(ROWS)
    same_chunk = (r[:, None] // CHUNK) == (r[None, :] // CHUNK)
    ltri = (same_chunk & (r[None, :] <= r[:, None])).astype(np.float32)
    u0 = np.zeros((ROWS, CPS * LANES), np.float32)
    u1 = np.zeros((ROWS, CPS * LANES), np.float32)
    for c in range(CPS):
        s_loc = np.arange(CHUNK)
        m = (s_loc[:, None] <= s_loc[None, :]).astype(np.float32)
        u0[c * CHUNK:(c + 1) * CHUNK, c * LANES:c * LANES + CHUNK] = m
        u1[c * CHUNK:(c + 1) * CHUNK, c * LANES + CHUNK:(c + 1) * LANES] = m
    e_ssd = np.zeros((SMALL_WIDTH, SSD_WIDTH), np.float32)
    for col, h in enumerate(_HEAD_PERM):
        e_ssd[col, h * SSD_HEAD_DIM:(h + 1) * SSD_HEAD_DIM] = 1.0
    e_beta = np.zeros((SMALL_WIDTH, DN_WIDTH), np.float32)
    e_g = np.zeros((SMALL_WIDTH, DN_WIDTH), np.float32)
    for h in range(DN_HEADS):
        e_beta[DNB_COL + h, h * DN_HEAD_DIM:(h + 1) * DN_HEAD_DIM] = 1.0
        e_g[DNA_COL + h, h * DN_HEAD_DIM:(h + 1) * DN_HEAD_DIM] = 1.0
    lane = np.arange(DN_WIDTH)
    bd128 = ((lane[:, None] // DN_HEAD_DIM) == (lane[None, :] // DN_HEAD_DIM)).astype(np.float32)
    bd64 = ((lane[:, None] // ATTN_HEAD_DIM) == (lane[None, :] // ATTN_HEAD_DIM)).astype(np.float32)
    bd64 = bd64 / ATTN_HEAD_DIM
    out = dict(ltri=ltri, u0=u0, u1=u1, e_ssd=e_ssd, e_beta=e_beta, e_g=e_g, bd128=bd128, bd64=bd64)
    return {k: jnp.asarray(v, BF16) for k, v in out.items()}


def _attn_bias(rel_bias):
    r = np.arange(ATTN_TQ)[:, None]
    c = np.arange(ATTN_NK)[None, :]
    rel = np.clip(r + ATTN_PAD - c, -ATTN_REL_CLIP, ATTN_REL_CLIP) + ATTN_REL_CLIP
    qc, kc = r // CHUNK, c // CHUNK
    valid = (kc >= qc) & (kc <= qc + ATTN_BAND_CHUNKS - 1)
    bias = rel_bias[:, rel]
    return jnp.where(valid[None], bias, NEG).astype(F32)


def _pad_row(v, start):
    return jnp.zeros((1, SMALL_WIDTH), F32).at[0, start:start + v.shape[0]].set(v.astype(F32))


def kernel(x, norm_g, w_in, ssd_conv_w, ssd_conv_b, ssd_dt_bias, ssd_a_log, ssd_d, ssd_norm_g,
           attn_q_norm_g, attn_k_norm_g, attn_rel_bias, dn_conv_w, dn_dt_bias, dn_a_log,
           dn_norm_g, w_out):
    nb, ns, _ = x.shape
    depth = w_in.shape[0]
    assert ns % ROWS == 0 and ns % ATTN_PAD == 0
    consts = _constants()
    x2 = x.reshape(nb * ns, D_MODEL)
    for l in range(depth):
        w = w_in[l]
        w_main = jnp.concatenate(
            [w[:, 1024:2560], w[:, 4624:6160], w[:, 0:1024], w[:, 2576:4624], w[:, 6160:6672]],
            axis=1).astype(BF16)
        w_small = jnp.concatenate(
            [w[:, 2560:2576][:, _HEAD_PERM], w[:, 6672:6680],
             jnp.zeros((D_MODEL, SMALL_WIDTH - 24), F32)], axis=1).astype(BF16)
        main, small = _inproj(x2, norm_g[l][None, :], w_main, w_small)

        y_ssd = _ssd(main, small, nb, ns, ssd_conv_w[l], ssd_conv_b[l][None, :],
                     _pad_row(ssd_dt_bias[l][_HEAD_PERM], 0), _pad_row(ssd_a_log[l][_HEAD_PERM], 0),
                     jnp.repeat(ssd_d[l].astype(F32), SSD_HEAD_DIM)[None, :],
                     ssd_norm_g[l][None, :], consts)

        qn, kn, vn = _aprep(main, nb, ns, jnp.tile(attn_q_norm_g[l], ATTN_HEADS)[None, :],
                            jnp.tile(attn_k_norm_g[l], ATTN_HEADS)[None, :], consts["bd64"])
        y_attn = _attn(qn, kn, vn, main, nb, ns, _attn_bias(attn_rel_bias[l]))

        y_dn = _gdn(main, small, nb, ns, dn_conv_w[l], _pad_row(dn_dt_bias[l], DNA_COL),
                    _pad_row(dn_a_log[l], DNA_COL), jnp.tile(dn_norm_g[l], DN_HEADS)[None, :], consts)

        wo = w_out[l].astype(BF16)
        x2 = _outproj(x2, y_ssd, y_attn, y_dn, wo[0:1024], wo[1024:1536], wo[1536:2048])
    return x2.reshape(nb, ns, D_MODEL)
```

```python
import functools

import numpy as np
import jax
import jax.numpy as jnp
from jax import lax
from jax.experimental import pallas as pl
from jax.experimental.pallas import tpu as pltpu

F32 = jnp.float32
BF16 = jnp.bfloat16

D_MODEL = 1024
CHUNK = 64
EPS = 1e-6
CONV_K = 4
SSD_WIDTH = 1024
SSD_HEADS = 16
SSD_HEAD_DIM = 64
SSD_GROUPS = 2
SSD_STATE = 128
SSD_CONV_DIM = 1536
ATTN_WIDTH = 512
ATTN_HEADS = 8
ATTN_HEAD_DIM = 64
ATTN_BAND_CHUNKS = 9
ATTN_REL_CLIP = 128
DN_WIDTH = 512
DN_HEADS = 4
DN_HEAD_DIM = 128
DN_CONV_DIM = 1536
MIX_WIDTH = 2048

MAIN_WIDTH = 6656
XBC_BLK = 0
DNQKV_BLK = 1
Z_BLK = 3
AQ_BLK, AK_BLK, AV_BLK, AGATE_BLK = 8, 9, 10, 11
DNGATE_BLK = 12
SMALL_WIDTH = 128
DNB_COL = 16
DNA_COL = 20

LANES = 128
NEG = -1e30
VMEM_LIMIT = 56 * 1024 * 1024

ROWS = 256
CPS = ROWS // CHUNK
ATTN_TQ = 128
ATTN_NK = ATTN_TQ + (ATTN_BAND_CHUNKS - 1) * CHUNK
ATTN_PAD = (ATTN_BAND_CHUNKS - 1) * CHUNK


def _dot(a, b):
    return jnp.dot(a, b, preferred_element_type=F32)


def _dot_nt(a, b):
    return lax.dot_general(a, b, (((1,), (1,)), ((), ())), preferred_element_type=F32)


def _split(v):
    hi = v.astype(BF16)
    lo = (v - hi.astype(F32)).astype(BF16)
    return hi, lo


def _dot2(v, m):
    hi, lo = _split(v)
    return _dot(hi, m) + _dot(lo, m)


def _dot2l(m, v):
    hi, lo = _split(v)
    return _dot(m, hi) + _dot(m, lo)


def _silu(x):
    return x * jax.nn.sigmoid(x)


def _softplus(x):
    return jnp.maximum(x, 0.0) + jnp.log1p(jnp.exp(-jnp.abs(x)))


def _conv_silu(x_ref, xe_ref, cw_ref, bias, first, rows):
    width = xe_ref.shape[1]

    @pl.when(first)
    def _():
        xe_ref[0:8, :] = jnp.zeros((8, width), F32)

    xe_ref[8:rows + 8, :] = x_ref[...].astype(F32)
    acc = cw_ref[0:1, :] * xe_ref[5:5 + rows, :]
    for t in range(1, CONV_K):
        acc = acc + cw_ref[t:t + 1, :] * xe_ref[5 + t:5 + t + rows, :]
    if bias is not None:
        acc = acc + bias
    xe_ref[0:8, :] = xe_ref[rows:rows + 8, :]
    return _silu(acc)


def _inproj_kernel(x_ref, g_ref, w_ref, ws_ref, main_ref, small_ref, h_ref):
    @pl.when(pl.program_id(1) == 0)
    def _():
        x = x_ref[...]
        ms = jnp.mean(x * x, axis=-1, keepdims=True)
        h = (x * lax.rsqrt(ms + EPS) * g_ref[...]).astype(BF16)
        h_ref[...] = h
        small_ref[...] = _dot(h, ws_ref[...])

    main_ref[...] = _dot(h_ref[...], w_ref[...]).astype(BF16)


def _inproj(x2, g, w_main, w_small):
    t = x2.shape[0]
    tm = min(1024, t)
    tn = MAIN_WIDTH // 2
    return pl.pallas_call(
        _inproj_kernel,
        grid=(t // tm, MAIN_WIDTH // tn),
        in_specs=[
            pl.BlockSpec((tm, D_MODEL), lambda i, j: (i, 0)),
            pl.BlockSpec((1, D_MODEL), lambda i, j: (0, 0)),
            pl.BlockSpec((D_MODEL, tn), lambda i, j: (0, j)),
            pl.BlockSpec((D_MODEL, SMALL_WIDTH), lambda i, j: (0, 0)),
        ],
        out_specs=[
            pl.BlockSpec((tm, tn), lambda i, j: (i, j)),
            pl.BlockSpec((tm, SMALL_WIDTH), lambda i, j: (i, 0)),
        ],
        out_shape=[
            jax.ShapeDtypeStruct((t, MAIN_WIDTH), BF16),
            jax.ShapeDtypeStruct((t, SMALL_WIDTH), F32),
        ],
        scratch_shapes=[pltpu.VMEM((tm, D_MODEL), BF16)],
        compiler_params=pltpu.CompilerParams(
            dimension_semantics=("arbitrary", "arbitrary"), vmem_limit_bytes=VMEM_LIMIT),
        name="inproj",
    )(x2, g, w_main, w_small)


def _outproj_kernel(x_ref, ys_ref, ya_ref, yd_ref, ws_ref, wa_ref, wd_ref, o_ref):
    acc = _dot(ys_ref[...], ws_ref[...])
    acc = acc + _dot(ya_ref[...], wa_ref[...])
    acc = acc + _dot(yd_ref[...], wd_ref[...])
    o_ref[...] = x_ref[...] + acc


def _outproj(x2, y_ssd, y_attn, y_dn, wo_s, wo_a, wo_d):
    t = x2.shape[0]
    tm = min(512, t)
    row = lambda w: pl.BlockSpec((tm, w), lambda i: (i, 0))
    full = lambda r: pl.BlockSpec((r, D_MODEL), lambda i: (0, 0))
    return pl.pallas_call(
        _outproj_kernel,
        grid=(t // tm,),
        in_specs=[row(D_MODEL), row(SSD_WIDTH), row(ATTN_WIDTH), row(DN_WIDTH),
                  full(SSD_WIDTH), full(ATTN_WIDTH), full(DN_WIDTH)],
        out_specs=row(D_MODEL),
        out_shape=jax.ShapeDtypeStruct((t, D_MODEL), F32),
        compiler_params=pltpu.CompilerParams(
            dimension_semantics=("arbitrary",), vmem_limit_bytes=VMEM_LIMIT),
        name="outproj",
    )(x2, y_ssd, y_attn, y_dn, wo_s, wo_a, wo_d)


def _ssd_kernel(xbc_ref, z_ref, sm_ref, cw_ref, cb_ref, dtb_ref, alog_ref, dexp_ref, ng_ref,
                e_ref, ltri_ref, u0_ref, u1_ref, y_ref, xe_ref, st_ref):
    first = pl.program_id(1) == 0

    @pl.when(first)
    def _():
        st_ref[...] = jnp.zeros_like(st_ref)

    xa = _conv_silu(xbc_ref, xe_ref, cw_ref, cb_ref[...], first, ROWS)
    xs = xa[:, 0:SSD_WIDTH]
    bm = xa[:, SSD_WIDTH:SSD_WIDTH + 256]
    cm = xa[:, SSD_WIDTH + 256:SSD_WIDTH + 512]

    dt = _softplus(sm_ref[...] + dtb_ref[...])
    d_a = dt * (-jnp.exp(alog_ref[...]))
    acum = _dot2l(ltri_ref[...], d_a)
    d_at = d_a.T[0:16, :]
    q0 = _dot2(d_at, u0_ref[...])
    q1 = _dot2(d_at, u1_ref[...])
    prow = q0[0:8, :] + q1[8:16, :]

    e = e_ref[...]
    acum_x = _dot2(acum, e)
    dt_x = _dot2(dt, e)
    xdt = xs * dt_x
    eacum = jnp.exp(acum_x)

    row_i = lax.broadcasted_iota(jnp.int32, (CHUNK, LANES), 0)
    lane = lax.broadcasted_iota(jnp.int32, (CHUNK, LANES), 1)
    lane_lo = lane < CHUNK
    tri2 = row_i >= jnp.where(lane_lo, lane, lane - CHUNK)

    ys = []
    for ci in range(CPS):
        rs = slice(ci * CHUNK, (ci + 1) * CHUNK)
        ax = acum_x[rs]
        last = ax[CHUNK - 1:CHUNK, :]
        dec = jnp.exp(last - ax)
        cdec = jnp.exp(last)
        xdt_c = xdt[rs]
        xdd = (xdt_c * dec).astype(BF16)
        xdt_b = xdt_c.astype(BF16)
        pc = prow[:, ci * LANES:(ci + 1) * LANES]
        ydiag, yoff = [], []
        for g in range(SSD_GROUPS):
            gs = slice(g * 512, (g + 1) * 512)
            bg = bm[rs, g * SSD_STATE:(g + 1) * SSD_STATE]
            cg = cm[rs, g * SSD_STATE:(g + 1) * SSD_STATE].astype(BF16)
            bgt2 = jnp.concatenate([bg, bg], axis=0).T.astype(BF16)
            cb2 = _dot(cg, bgt2)
            st_g = st_ref[:, gs]
            yoff.append(_dot(cg, st_g.astype(BF16)))
            new_g = _dot(bgt2[:, 0:CHUNK], xdd[:, gs])
            st_ref[:, gs] = st_g * cdec[:, gs] + new_g
            for jp in range(4):
                j = g * 4 + jp
                tl = slice(j * LANES, (j + 1) * LANES)
                diff = ax[:, tl] - pc[j:j + 1, :]
                lp = jnp.exp(jnp.where(tri2, diff, NEG))
                mp = cb2 * lp
                m_lo = jnp.where(lane_lo, mp, 0.0).astype(BF16)
                m_hi = jnp.where(lane_lo, 0.0, mp).astype(BF16)
                xt = xdt_b[:, tl]
                x2 = jnp.concatenate([xt, xt], axis=0)
                ydiag.append(jnp.where(lane_lo, _dot(m_lo, x2), _dot(m_hi, x2)))
        y_c = (jnp.concatenate(ydiag, axis=1) + jnp.concatenate(yoff, axis=1) * eacum[rs]
               + xs[rs] * dexp_ref[...])
        ys.append(y_c)
    y = jnp.concatenate(ys, axis=0) * _silu(z_ref[...].astype(F32))
    outs = []
    for g in range(SSD_GROUPS):
        gs = slice(g * 512, (g + 1) * 512)
        yg = y[:, gs]
        ms = jnp.mean(yg * yg, axis=-1, keepdims=True)
        outs.append(yg * lax.rsqrt(ms + EPS) * ng_ref[:, gs])
    y_ref[...] = jnp.concatenate(outs, axis=1).astype(BF16)


def _ssd(main, small, nb, ns, cw, cb, dtb, alog, dexp, ng, consts):
    t = main.shape[0]
    spb = ns // ROWS
    rowmap = lambda blk: (lambda b, c: (b * spb + c, blk))
    const = lambda shp: pl.BlockSpec(shp, lambda b, c: (0, 0))
    return pl.pallas_call(
        _ssd_kernel,
        grid=(nb, spb),
        in_specs=[
            pl.BlockSpec((ROWS, SSD_CONV_DIM), rowmap(XBC_BLK)),
            pl.BlockSpec((ROWS, SSD_WIDTH), rowmap(Z_BLK)),
            pl.BlockSpec((ROWS, SMALL_WIDTH), rowmap(0)),
            const((CONV_K, SSD_CONV_DIM)), const((1, SSD_CONV_DIM)),
            const((1, SMALL_WIDTH)), const((1, SMALL_WIDTH)),
            const((1, SSD_WIDTH)), const((1, SSD_WIDTH)),
            const((SMALL_WIDTH, SSD_WIDTH)), const((ROWS, ROWS)),
            const((ROWS, CPS * LANES)), const((ROWS, CPS * LANES)),
        ],
        out_specs=pl.BlockSpec((ROWS, SSD_WIDTH), rowmap(0)),
        out_shape=jax.ShapeDtypeStruct((t, SSD_WIDTH), BF16),
        scratch_shapes=[pltpu.VMEM((ROWS + 8, SSD_CONV_DIM), F32),
                        pltpu.VMEM((SSD_STATE, SSD_WIDTH), F32)],
        compiler_params=pltpu.CompilerParams(
            dimension_semantics=("arbitrary", "arbitrary"), vmem_limit_bytes=VMEM_LIMIT),
        name="ssd",
    )(main, main, small, cw, cb, dtb, alog, dexp, ng,
      consts["e_ssd"], consts["ltri"], consts["u0"], consts["u1"])


def _aprep_kernel(q_ref, k_ref, v_ref, gq_ref, gk_ref, bd_ref, qo_ref, ko_ref, vo_ref):
    bd = bd_ref[...]
    q = q_ref[...].astype(F32)
    qms = _dot((q * q).astype(BF16), bd)
    qo_ref[...] = (q * lax.rsqrt(qms + EPS) * gq_ref[...] * (ATTN_HEAD_DIM ** -0.5)).astype(BF16)
    pad = pl.program_id(1) == 0

    @pl.when(pad)
    def _():
        ko_ref[...] = jnp.zeros_like(ko_ref)
        vo_ref[...] = jnp.zeros_like(vo_ref)

    @pl.when(jnp.logical_not(pad))
    def _():
        k = k_ref[...].astype(F32)
        kms = _dot((k * k).astype(BF16), bd)
        ko_ref[...] = (k * lax.rsqrt(kms + EPS) * gk_ref[...]).astype(BF16)
        vo_ref[...] = v_ref[...]


def _aprep(main, nb, ns, gq, gk, bd64):
    t = main.shape[0]
    rb = ATTN_PAD
    spb = ns // rb
    src = lambda blk: (lambda b, i: (b * spb + jnp.maximum(i - 1, 0), blk))
    const = lambda shp: pl.BlockSpec(shp, lambda b, i: (0, 0))
    tp = nb * (ns + ATTN_PAD)
    return pl.pallas_call(
        _aprep_kernel,
        grid=(nb, spb + 1),
        in_specs=[
            pl.BlockSpec((rb, ATTN_WIDTH), src(AQ_BLK)),
            pl.BlockSpec((rb, ATTN_WIDTH), src(AK_BLK)),
            pl.BlockSpec((rb, ATTN_WIDTH), src(AV_BLK)),
            const((1, ATTN_WIDTH)), const((1, ATTN_WIDTH)), const((ATTN_WIDTH, ATTN_WIDTH)),
        ],
        out_specs=[
            pl.BlockSpec((rb, ATTN_WIDTH), lambda b, i: (b * spb + jnp.maximum(i - 1, 0), 0)),
            pl.BlockSpec((rb, ATTN_WIDTH), lambda b, i: (b * (spb + 1) + i, 0)),
            pl.BlockSpec((rb, ATTN_WIDTH), lambda b, i: (b * (spb + 1) + i, 0)),
        ],
        out_shape=[
            jax.ShapeDtypeStruct((t, ATTN_WIDTH), BF16),
            jax.ShapeDtypeStruct((tp, ATTN_WIDTH), BF16),
            jax.ShapeDtypeStruct((tp, ATTN_WIDTH), BF16),
        ],
        compiler_params=pltpu.CompilerParams(
            dimension_semantics=("arbitrary", "arbitrary"), vmem_limit_bytes=VMEM_LIMIT),
        name="attn_prep",
    )(main, main, main, gq, gk, bd64)


def _attn_kernel(q_ref, k_ref, v_ref, gate_ref, bias_ref, o_ref):
    qi = pl.program_id(1)
    start = pl.multiple_of(qi * ATTN_TQ, ATTN_TQ)
    kk = k_ref[pl.ds(start, ATTN_NK), :]
    vv = v_ref[pl.ds(start, ATTN_NK), :]
    q = q_ref[...]
    col = lax.broadcasted_iota(jnp.int32, (1, ATTN_NK), 1)
    padrow = jnp.where(col < ATTN_PAD - qi * ATTN_TQ, NEG, 0.0)
    lane_lo = lax.broadcasted_iota(jnp.int32, (ATTN_TQ, LANES), 1) < ATTN_HEAD_DIM
    outs = []
    for jp in range(ATTN_HEADS // 2):
        tl = slice(jp * LANES, (jp + 1) * LANES)
        qp, kp, vp = q[:, tl], kk[:, tl], vv[:, tl]
        o2 = []
        for s in range(2):
            keep = lane_lo if s == 0 else jnp.logical_not(lane_lo)
            qm = jnp.where(keep, qp, jnp.zeros_like(qp))
            sc = _dot_nt(qm, kp) + bias_ref[2 * jp + s] + padrow
            m = jnp.max(sc, axis=-1, keepdims=True)
            p = jnp.exp(sc - m)
            l = jnp.sum(p, axis=-1, keepdims=True)
            o2.append(_dot(p.astype(BF16), vp) / l)
        outs.append(jnp.where(lane_lo, o2[0], o2[1]))
    o = jnp.concatenate(outs, axis=1) * _silu(gate_ref[...].astype(F32))
    o_ref[...] = o.astype(BF16)


def _attn(qn, kn, vn, main, nb, ns, bias):
    t = qn.shape[0]
    spb = ns // ATTN_TQ
    kv_rows = ns + ATTN_PAD
    return pl.pallas_call(
        _attn_kernel,
        grid=(nb, spb),
        in_specs=[
            pl.BlockSpec((ATTN_TQ, ATTN_WIDTH), lambda b, i: (b * spb + i, 0)),
            pl.BlockSpec((kv_rows, ATTN_WIDTH), lambda b, i: (b, 0)),
            pl.BlockSpec((kv_rows, ATTN_WIDTH), lambda b, i: (b, 0)),
            pl.BlockSpec((ATTN_TQ, ATTN_WIDTH), lambda b, i: (b * spb + i, AGATE_BLK)),
            pl.BlockSpec((ATTN_HEADS, ATTN_TQ, ATTN_NK), lambda b, i: (0, 0, 0)),
        ],
        out_specs=pl.BlockSpec((ATTN_TQ, ATTN_WIDTH), lambda b, i: (b * spb + i, 0)),
        out_shape=jax.ShapeDtypeStruct((t, ATTN_WIDTH), BF16),
        compiler_params=pltpu.CompilerParams(
            dimension_semantics=("arbitrary", "arbitrary"), vmem_limit_bytes=VMEM_LIMIT),
        name="attn",
    )(qn, kn, vn, main, bias)


def _row_stack_masked(y, groups):
    z = jnp.zeros_like(y)
    return jnp.concatenate([jnp.where(g, y, z) for g in groups], axis=0)


def _tri_inv_minus_eye(a_list, eye, blk, grp):
    bf = lambda xs: [x.astype(BF16) for x in xs]
    bd = lambda xs: [_row_stack_masked(x, grp) for x in xs]
    mm = lambda xs, ys: [_dot(x, y) for x, y in zip(xs, ys)]
    add = lambda xs, ys: [x + y for x, y in zip(xs, ys)]
    d = [jnp.where(blk, a, 0.0) for a in a_list]
    o = [a - x for a, x in zip(a_list, d)]
    db = bf(d)
    d2b = bf(mm(db, bd(db)))
    d2bd = bd(d2b)
    d4b = bf(mm(d2b, d2bd))
    d4bd = bd(d4b)
    d8bd = bd(bf(mm(d4b, d4bd)))
    x = [eye - y for y in d]
    x = add(x, mm(bf(x), d2bd))
    x = add(x, mm(bf(x), d4bd))
    td = add(x, mm(bf(x), d8bd))
    tdm = [y - eye for y in td]
    tdmb = bf(tdm)
    n = add(o, mm(tdmb, bd(bf(o))))
    nb = bf(n)
    n2bd = bd(bf(mm(nb, bd(nb))))
    tnm = [m - y for m, y in zip(mm(bf([eye - y for y in n]), n2bd), n)]
    cross = mm(bf(tnm), bd(tdmb))
    return [p + q + r for p, q, r in zip(tdm, tnm, cross)]


def _gdn_kernel(qkv_ref, gate_ref, sm_ref, cw_ref, dtb_ref, alog_ref, ng_ref, eb_ref, eg_ref,
                eq_ref, ltri_ref, wcat_ref, bd_ref, o_ref, xe_ref, st_ref):
    first = pl.program_id(1) == 0

    @pl.when(first)
    def _():
        st_ref[...] = jnp.zeros_like(st_ref)

    qkv = _conv_silu(qkv_ref, xe_ref, cw_ref, None, first, ROWS)
    q = qkv[:, 0:DN_WIDTH]
    k = qkv[:, DN_WIDTH:2 * DN_WIDTH]
    v = qkv[:, 2 * DN_WIDTH:3 * DN_WIDTH]
    bd = bd_ref[...]
    q = q * lax.rsqrt(_dot((q * q).astype(BF16), bd) + EPS) * (DN_HEAD_DIM ** -0.5)
    k = k * lax.rsqrt(_dot((k * k).astype(BF16), bd) + EPS)

    sm = sm_ref[...]
    beta_s = jax.nn.sigmoid(sm)
    g_s = -jnp.exp(alog_ref[...]) * _softplus(sm + dtb_ref[...])
    gcum = _dot2l(ltri_ref[...], g_s)
    g_t = g_s.T
    g_rows = jnp.concatenate([g_t[DNA_COL + h:DNA_COL + h + 1, :] for h in range(DN_HEADS)], axis=1)
    grow = _dot2(jnp.broadcast_to(g_rows, (16, DN_HEADS * ROWS)), wcat_ref[...])[0:1, :]
    gcol = _dot2(gcum, eq_ref[...])
    beta_x = _dot2(beta_s, eb_ref[...])
    gc_x = _dot2(gcum, eg_ref[...])
    egc = jnp.exp(gc_x)
    kb = k * beta_x
    vb = v * beta_x
    kbg = kb * egc
    qd = q * egc

    qw = DN_HEADS * CHUNK
    ri = lax.broadcasted_iota(jnp.int32, (CHUNK, qw), 0)
    lq = lax.broadcasted_iota(jnp.int32, (CHUNK, qw), 1)
    cj = lq & (CHUNK - 1)
    tri = ri >= cj
    stri = ri > cj
    blk = (ri >> 4) == (cj >> 4)
    eye = jnp.where(ri == cj, 1.0, 0.0).astype(F32)
    grp_q = [(lq >> 6) == h for h in range(DN_HEADS)]
    lw = lax.broadcasted_iota(jnp.int32, (CHUNK, DN_WIDTH), 1)
    grp_w = [(lw >> 7) == h for h in range(DN_HEADS)]

    chunks = [slice(ci * CHUNK, (ci + 1) * CHUNK) for ci in range(CPS)]
    a_list, attn_list = [], []
    for ci, rs in enumerate(chunks):
        kstack = _row_stack_masked(k[rs].astype(BF16), grp_w)
        lhs = jnp.concatenate([kb[rs], q[rs]], axis=0).astype(BF16)
        kq = _dot_nt(lhs, kstack)
        diff = gcol[rs] - grow[:, ci * qw:(ci + 1) * qw]
        dm = jnp.exp(jnp.where(tri, diff, NEG))
        a_list.append(jnp.where(stri, kq[0:CHUNK] * dm, 0.0))
        attn_list.append((kq[CHUNK:2 * CHUNK] * dm).astype(BF16))
    tm_list = _tri_inv_minus_eye(a_list, eye, blk, grp_q)
    u_list, w_list = [], []
    for rs, tm in zip(chunks, tm_list):
        tmb = tm.astype(BF16)
        u_list.append(vb[rs] + _dot(tmb, _row_stack_masked(vb[rs].astype(BF16), grp_w)))
        w_list.append(kbg[rs] + _dot(tmb, _row_stack_masked(kbg[rs].astype(BF16), grp_w)))

    outs = []
    for ci, rs in enumerate(chunks):
        gx = gc_x[rs]
        glast = gx[CHUNK - 1:CHUNK, :]
        kd = k[rs] * jnp.exp(glast - gx)
        sdec = jnp.exp(glast)
        wq = []
        for h in range(DN_HEADS):
            th = slice(h * DN_HEAD_DIM, (h + 1) * DN_HEAD_DIM)
            lhs = jnp.concatenate([w_list[ci][:, th], qd[rs, th]], axis=0).astype(BF16)
            wq.append(_dot(lhs, st_ref[h].astype(BF16)))
        vn = u_list[ci] - jnp.concatenate([x[0:CHUNK] for x in wq], axis=1)
        vnb = vn.astype(BF16)
        outs.append(jnp.concatenate([x[CHUNK:2 * CHUNK] for x in wq], axis=1)
                    + _dot(attn_list[ci], _row_stack_masked(vnb, grp_w)))
        for h in range(DN_HEADS):
            th = slice(h * DN_HEAD_DIM, (h + 1) * DN_HEAD_DIM)
            st_ref[h] = st_ref[h] * sdec[:, th] + _dot(kd[:, th].T.astype(BF16), vnb[:, th])
    o = jnp.concatenate(outs, axis=0)
    ms = _dot((o * o).astype(BF16), bd) * (1.0 / DN_HEAD_DIM)
    o = o * lax.rsqrt(ms + EPS) * ng_ref[...]
    o_ref[...] = (o * _silu(gate_ref[...].astype(F32))).astype(BF16)


def _gdn(main, small, nb, ns, cw, dtb, alog, ng, consts):
    t = main.shape[0]
    spb = ns // ROWS
    rowmap = lambda blk: (lambda b, c: (b * spb + c, blk))
    const = lambda shp: pl.BlockSpec(shp, lambda b, c: (0, 0))
    return pl.pallas_call(
        _gdn_kernel,
        grid=(nb, spb),
        in_specs=[
            pl.BlockSpec((ROWS, DN_CONV_DIM), rowmap(DNQKV_BLK)),
            pl.BlockSpec((ROWS, DN_WIDTH), rowmap(DNGATE_BLK)),
            pl.BlockSpec((ROWS, SMALL_WIDTH), rowmap(0)),
            const((CONV_K, DN_CONV_DIM)),
            const((1, SMALL_WIDTH)), const((1, SMALL_WIDTH)), const((1, DN_WIDTH)),
            const((SMALL_WIDTH, DN_WIDTH)), const((SMALL_WIDTH, DN_WIDTH)),
            const((SMALL_WIDTH, DN_HEADS * CHUNK)),
            const((ROWS, ROWS)), const((DN_HEADS * ROWS, CPS * DN_HEADS * CHUNK)),
            const((DN_WIDTH, DN_WIDTH)),
        ],
        out_specs=pl.BlockSpec((ROWS, DN_WIDTH), rowmap(0)),
        out_shape=jax.ShapeDtypeStruct((t, DN_WIDTH), BF16),
        scratch_shapes=[pltpu.VMEM((ROWS + 8, DN_CONV_DIM), F32),
                        pltpu.VMEM((DN_HEADS, DN_HEAD_DIM, DN_HEAD_DIM), F32)],
        compiler_params=pltpu.CompilerParams(
            dimension_semantics=("arbitrary", "arbitrary"), vmem_limit_bytes=VMEM_LIMIT),
        name="gdn",
    )(main, main, small, cw, dtb, alog, ng,
      consts["e_beta"], consts["e_g"], consts["e_gq"], consts["ltri"], consts["wcat"],
      consts["bd128"])


_HEAD_PERM = np.concatenate([np.arange(0, SSD_HEADS, 2), np.arange(1, SSD_HEADS, 2)])


def _constants():
    r = np.arange(ROWS)
    same_chunk = (r[:, None] // CHUNK) == (r[None, :] // CHUNK)
    ltri = (same_chunk & (r[None, :] <= r[:, None])).astype(np.float32)
    u0 = np.zeros((ROWS, CPS * LANES), np.float32)
    u1 = np.zeros((ROWS, CPS * LANES), np.float32)
    for c in range(CPS):
        s_loc = np.arange(CHUNK)
        m = (s_loc[:, None] <= s_loc[None, :]).astype(np.float32)
        u0[c * CHUNK:(c + 1) * CHUNK, c * LANES:c * LANES + CHUNK] = m
        u1[c * CHUNK:(c + 1) * CHUNK, c * LANES + CHUNK:(c + 1) * LANES] = m
    e_ssd = np.zeros((SMALL_WIDTH, SSD_WIDTH), np.float32)
    for col, h in enumerate(_HEAD_PERM):
        e_ssd[col, h * SSD_HEAD_DIM:(h + 1) * SSD_HEAD_DIM] = 1.0
    e_beta = np.zeros((SMALL_WIDTH, DN_WIDTH), np.float32)
    e_g = np.zeros((SMALL_WIDTH, DN_WIDTH), np.float32)
    qw = DN_HEADS * CHUNK
    e_gq = np.zeros((SMALL_WIDTH, qw), np.float32)
    wcat = np.zeros((DN_HEADS * ROWS, CPS * qw), np.float32)
    incl = (np.arange(CHUNK)[:, None] <= np.arange(CHUNK)[None, :]).astype(np.float32)
    for h in range(DN_HEADS):
        e_beta[DNB_COL + h, h * DN_HEAD_DIM:(h + 1) * DN_HEAD_DIM] = 1.0
        e_g[DNA_COL + h, h * DN_HEAD_DIM:(h + 1) * DN_HEAD_DIM] = 1.0
        e_gq[DNA_COL + h, h * CHUNK:(h + 1) * CHUNK] = 1.0
        for c in range(CPS):
            wcat[h * ROWS + c * CHUNK:h * ROWS + (c + 1) * CHUNK,
                 c * qw + h * CHUNK:c * qw + (h + 1) * CHUNK] = incl
    lane = np.arange(DN_WIDTH)
    bd128 = ((lane[:, None] // DN_HEAD_DIM) == (lane[None, :] // DN_HEAD_DIM)).astype(np.float32)
    bd64 = ((lane[:, None] // ATTN_HEAD_DIM) == (lane[None, :] // ATTN_HEAD_DIM)).astype(np.float32)
    bd64 = bd64 / ATTN_HEAD_DIM
    out = dict(ltri=ltri, u0=u0, u1=u1, e_ssd=e_ssd, e_beta=e_beta, e_g=e_g, e_gq=e_gq, wcat=wcat,
               bd128=bd128, bd64=bd64)
    return {k: jnp.asarray(v, BF16) for k, v in out.items()}


def _attn_bias(rel_bias):
    r = np.arange(ATTN_TQ)[:, None]
    c = np.arange(ATTN_NK)[None, :]
    qc, kc = r // CHUNK, c // CHUNK
    valid = (kc >= qc) & (kc <= qc + ATTN_BAND_CHUNKS - 1)
    nh = rel_bias.shape[0]
    assert ATTN_TQ - 1 <= ATTN_REL_CLIP
    near = rel_bias[:, ATTN_REL_CLIP - (ATTN_TQ - 1):2 * ATTN_REL_CLIP]
    far = jnp.broadcast_to(rel_bias[:, 2 * ATTN_REL_CLIP:], (nh, ATTN_NK - ATTN_REL_CLIP))
    t = jnp.concatenate([near, far], axis=1)
    span = ATTN_TQ + ATTN_NK - 1
    w = jnp.concatenate([t[:, ::-1], jnp.zeros((nh, 1), rel_bias.dtype)], axis=1)
    skew = jnp.tile(w, (1, ATTN_TQ))[:, :ATTN_TQ * span].reshape(nh, ATTN_TQ, span)
    bias = skew[:, :, ATTN_TQ - 1:ATTN_TQ - 1 + ATTN_NK]
    return jnp.where(valid[None], bias, NEG).astype(F32)


def _pad_row(v, start):
    return jnp.zeros((1, SMALL_WIDTH), F32).at[0, start:start + v.shape[0]].set(v.astype(F32))


def kernel(x, norm_g, w_in, ssd_conv_w, ssd_conv_b, ssd_dt_bias, ssd_a_log, ssd_d, ssd_norm_g,
           attn_q_norm_g, attn_k_norm_g, attn_rel_bias, dn_conv_w, dn_dt_bias, dn_a_log,
           dn_norm_g, w_out):
    nb, ns, _ = x.shape
    depth = w_in.shape[0]
    assert ns % ROWS == 0 and ns % ATTN_PAD == 0
    consts = _constants()
    x2 = x.reshape(nb * ns, D_MODEL)
    for l in range(depth):
        w = w_in[l]
        w_main = jnp.concatenate(
            [w[:, 1024:2560], w[:, 4624:6160], w[:, 0:1024], w[:, 2576:4624], w[:, 6160:6672]],
            axis=1).astype(BF16)
        w_small = jnp.concatenate(
            [w[:, 2560:2576][:, _HEAD_PERM], w[:, 6672:6680],
             jnp.zeros((D_MODEL, SMALL_WIDTH - 24), F32)], axis=1).astype(BF16)
        main, small = _inproj(x2, norm_g[l][None, :], w_main, w_small)

        y_ssd = _ssd(main, small, nb, ns, ssd_conv_w[l], ssd_conv_b[l][None, :],
                     _pad_row(ssd_dt_bias[l][_HEAD_PERM], 0), _pad_row(ssd_a_log[l][_HEAD_PERM], 0),
                     jnp.repeat(ssd_d[l].astype(F32), SSD_HEAD_DIM)[None, :],
                     ssd_norm_g[l][None, :], consts)

        qn, kn, vn = _aprep(main, nb, ns, jnp.tile(attn_q_norm_g[l], ATTN_HEADS)[None, :],
                            jnp.tile(attn_k_norm_g[l], ATTN_HEADS)[None, :], consts["bd64"])
        y_attn = _attn(qn, kn, vn, main, nb, ns, _attn_bias(attn_rel_bias[l]))

        y_dn = _gdn(main, small, nb, ns, dn_conv_w[l], _pad_row(dn_dt_bias[l], DNA_COL),
                    _pad_row(dn_a_log[l], DNA_COL), jnp.tile(dn_norm_g[l], DN_HEADS)[None, :], consts)

        wo = w_out[l].astype(BF16)
        x2 = _outproj(x2, y_ssd, y_attn, y_dn, wo[0:1024], wo[1024:1536], wo[1536:2048])
    return x2.reshape(nb, ns, D_MODEL)
```

```python
import functools

import numpy as np
import jax
import jax.numpy as jnp
from jax import lax
from jax.experimental import pallas as pl
from jax.experimental.pallas import tpu as pltpu

F32 = jnp.float32
BF16 = jnp.bfloat16

D_MODEL = 1024
CHUNK = 64
EPS = 1e-6
CONV_K = 4
SSD_WIDTH = 1024
SSD_HEADS = 16
SSD_HEAD_DIM = 64
SSD_GROUPS = 2
SSD_STATE = 128
SSD_CONV_DIM = 1536
ATTN_WIDTH = 512
ATTN_HEADS = 8
ATTN_HEAD_DIM = 64
ATTN_BAND_CHUNKS = 9
ATTN_REL_CLIP = 128
DN_WIDTH = 512
DN_HEADS = 4
DN_HEAD_DIM = 128
DN_CONV_DIM = 1536
MIX_WIDTH = 2048

MAIN_WIDTH = 6656
XBC_BLK = 0
DNQKV_BLK = 1
Z_BLK = 3
AQ_BLK, AK_BLK, AV_BLK, AGATE_BLK = 8, 9, 10, 11
DNGATE_BLK = 12
SMALL_WIDTH = 128
DNB_COL = 16
DNA_COL = 20

LANES = 128
NEG = -1e30
VMEM_LIMIT = 56 * 1024 * 1024

ROWS = 256
CPS = ROWS // CHUNK
TAIL = 8
ATTN_TQ = 128
ATTN_NK = ATTN_TQ + (ATTN_BAND_CHUNKS - 1) * CHUNK
ATTN_PAD = (ATTN_BAND_CHUNKS - 1) * CHUNK
PAIR_W = 2 * DN_HEAD_DIM


def _dot(a, b):
    return jnp.dot(a, b, preferred_element_type=F32)


def _dot_nt(a, b):
    return lax.dot_general(a, b, (((1,), (1,)), ((), ())), preferred_element_type=F32)


def _split(v):
    hi = v.astype(BF16)
    lo = (v - hi.astype(F32)).astype(BF16)
    return hi, lo


def _dot2(v, m):
    hi, lo = _split(v)
    return _dot(hi, m) + _dot(lo, m)


def _dot2k(v, m2):
    hi, lo = _split(v)
    return _dot(jnp.concatenate([hi, lo], axis=1), m2)


def _dot2l(m, v):
    hi, lo = _split(v)
    return _dot(m, hi) + _dot(m, lo)


def _silu(x):
    return x * jax.nn.sigmoid(x)


def _softplus(x):
    return jnp.maximum(x, 0.0) + jnp.log1p(jnp.exp(-jnp.abs(x)))


def _conv_silu(xb, tail_ref, seq, shift_ref, cw_ref, bias, first):
    rows = xb.shape[0]

    @pl.when(first)
    def _():
        tail_ref[seq] = jnp.zeros(tail_ref.shape[1:], tail_ref.dtype)

    tail = tail_ref[seq]
    xf = xb.astype(F32)
    acc = cw_ref[CONV_K - 1:CONV_K, :] * xf
    row = lax.broadcasted_iota(jnp.int32, tail.shape, 0)
    head = None
    for t in range(CONV_K - 1):
        s = CONV_K - 1 - t
        w_t = cw_ref[t:t + 1, :]
        acc = acc + w_t * _dot(shift_ref[t], xb)
        corr = w_t * jnp.where(row < s, pltpu.roll(tail, s, 0), 0.0)
        head = corr if head is None else head + corr
    acc = jnp.concatenate([acc[0:TAIL] + head, acc[TAIL:]], axis=0)
    if bias is not None:
        acc = acc + bias
    tail_ref[seq] = xf[rows - TAIL:rows, :]
    return _silu(acc)


def _nseq(nb):
    return 2 if nb % 2 == 0 else 1


def _inproj_kernel(x_ref, g_ref, w_ref, ws_ref, main_ref, small_ref, h_ref):
    @pl.when(pl.program_id(1) == 0)
    def _():
        x = x_ref[...]
        ms = jnp.mean(x * x, axis=-1, keepdims=True)
        h = (x * lax.rsqrt(ms + EPS) * g_ref[...]).astype(BF16)
        h_ref[...] = h
        small_ref[...] = _dot(h, ws_ref[...])

    main_ref[...] = _dot(h_ref[...], w_ref[...]).astype(BF16)


def _inproj(x2, g, w_main, w_small):
    t = x2.shape[0]
    tm = min(1024, t)
    tn = MAIN_WIDTH // 2
    return pl.pallas_call(
        _inproj_kernel,
        grid=(t // tm, MAIN_WIDTH // tn),
        in_specs=[
            pl.BlockSpec((tm, D_MODEL), lambda i, j: (i, 0)),
            pl.BlockSpec((1, D_MODEL), lambda i, j: (0, 0)),
            pl.BlockSpec((D_MODEL, tn), lambda i, j: (0, j)),
            pl.BlockSpec((D_MODEL, SMALL_WIDTH), lambda i, j: (0, 0)),
        ],
        out_specs=[
            pl.BlockSpec((tm, tn), lambda i, j: (i, j)),
            pl.BlockSpec((tm, SMALL_WIDTH), lambda i, j: (i, 0)),
        ],
        out_shape=[
            jax.ShapeDtypeStruct((t, MAIN_WIDTH), BF16),
            jax.ShapeDtypeStruct((t, SMALL_WIDTH), F32),
        ],
        scratch_shapes=[pltpu.VMEM((tm, D_MODEL), BF16)],
        compiler_params=pltpu.CompilerParams(
            dimension_semantics=("arbitrary", "arbitrary"), vmem_limit_bytes=VMEM_LIMIT),
        name="inproj",
    )(x2, g, w_main, w_small)


def _outproj_kernel(x_ref, ys_ref, ya_ref, yd_ref, ws_ref, wa_ref, wd_ref, o_ref):
    acc = _dot(ys_ref[...], ws_ref[...])
    acc = acc + _dot(ya_ref[...], wa_ref[...])
    acc = acc + _dot(yd_ref[...], wd_ref[...])
    o_ref[...] = x_ref[...] + acc


def _outproj(x2, y_ssd, y_attn, y_dn, wo_s, wo_a, wo_d):
    t = x2.shape[0]
    tm = min(512, t)
    row = lambda w: pl.BlockSpec((tm, w), lambda i: (i, 0))
    full = lambda r: pl.BlockSpec((r, D_MODEL), lambda i: (0, 0))
    return pl.pallas_call(
        _outproj_kernel,
        grid=(t // tm,),
        in_specs=[row(D_MODEL), row(SSD_WIDTH), row(ATTN_WIDTH), row(DN_WIDTH),
                  full(SSD_WIDTH), full(ATTN_WIDTH), full(DN_WIDTH)],
        out_specs=row(D_MODEL),
        out_shape=jax.ShapeDtypeStruct((t, D_MODEL), F32),
        compiler_params=pltpu.CompilerParams(
            dimension_semantics=("arbitrary",), vmem_limit_bytes=VMEM_LIMIT),
        name="outproj",
    )(x2, y_ssd, y_attn, y_dn, wo_s, wo_a, wo_d)


def _ssd_seq(seq, first, xbc_ref, z_ref, sm_ref, cw_ref, cb_ref, dtb_ref, alog_ref, dexp_ref, ng_ref,
             e2_ref, ltri_ref, u0_ref, u1_ref, shift_ref, y_ref, tail_ref, st_ref):
    xa = _conv_silu(xbc_ref[seq], tail_ref, seq, shift_ref, cw_ref, cb_ref[...], first)
    xs = xa[:, 0:SSD_WIDTH]
    bm = xa[:, SSD_WIDTH:SSD_WIDTH + 256]
    cm = xa[:, SSD_WIDTH + 256:SSD_WIDTH + 512]

    dt = _softplus(sm_ref[seq] + dtb_ref[...])
    d_a = dt * (-jnp.exp(alog_ref[...]))
    acum = _dot2l(ltri_ref[...], d_a)
    d_at = d_a.T[0:16, :]
    q0 = _dot2(d_at, u0_ref[...])
    q1 = _dot2(d_at, u1_ref[...])
    prow = q0[0:8, :] + q1[8:16, :]

    e2 = e2_ref[...]
    acum_x = _dot2k(acum, e2)
    dt_x = _dot2k(dt, e2)
    xdt = xs * dt_x
    eacum = jnp.exp(acum_x)

    row_i = lax.broadcasted_iota(jnp.int32, (CHUNK, LANES), 0)
    lane = lax.broadcasted_iota(jnp.int32, (CHUNK, LANES), 1)
    lane_lo = lane < CHUNK
    tri2 = row_i >= jnp.where(lane_lo, lane, lane - CHUNK)

    ys = []
    for ci in range(CPS):
        rs = slice(ci * CHUNK, (ci + 1) * CHUNK)
        ax = acum_x[rs]
        last = ax[CHUNK - 1:CHUNK, :]
        dec = jnp.exp(last - ax)
        cdec = jnp.exp(last)
        xdt_c = xdt[rs]
        xdd = (xdt_c * dec).astype(BF16)
        xdt_b = xdt_c.astype(BF16)
        pc = prow[:, ci * LANES:(ci + 1) * LANES]
        ydiag, yoff = [], []
        for g in range(SSD_GROUPS):
            gs = slice(g * 512, (g + 1) * 512)
            bg = bm[rs, g * SSD_STATE:(g + 1) * SSD_STATE]
            cg = cm[rs, g * SSD_STATE:(g + 1) * SSD_STATE].astype(BF16)
            bgt2 = jnp.concatenate([bg, bg], axis=0).T.astype(BF16)
            cb2 = _dot(cg, bgt2)
            st_g = st_ref[seq, :, gs]
            yoff.append(_dot(cg, st_g.astype(BF16)))
            new_g = _dot(bgt2[:, 0:CHUNK], xdd[:, gs])
            st_ref[seq, :, gs] = st_g * cdec[:, gs] + new_g
            for jp in range(4):
                j = g * 4 + jp
                tl = slice(j * LANES, (j + 1) * LANES)
                diff = ax[:, tl] - pc[j:j + 1, :]
                lp = jnp.exp(jnp.where(tri2, diff, NEG))
                mp = cb2 * lp
                m_lo = jnp.where(lane_lo, mp, 0.0).astype(BF16)
                m_hi = jnp.where(lane_lo, 0.0, mp).astype(BF16)
                xt = xdt_b[:, tl]
                x2 = jnp.concatenate([xt, xt], axis=0)
                ydiag.append(jnp.where(lane_lo, _dot(m_lo, x2), _dot(m_hi, x2)))
        y_c = (jnp.concatenate(ydiag, axis=1) + jnp.concatenate(yoff, axis=1) * eacum[rs]
               + xs[rs] * dexp_ref[...])
        ys.append(y_c)
    y = jnp.concatenate(ys, axis=0) * _silu(z_ref[seq].astype(F32))
    outs = []
    for g in range(SSD_GROUPS):
        gs = slice(g * 512, (g + 1) * 512)
        yg = y[:, gs]
        ms = jnp.mean(yg * yg, axis=-1, keepdims=True)
        outs.append(yg * lax.rsqrt(ms + EPS) * ng_ref[:, gs])
    y_ref[seq] = jnp.concatenate(outs, axis=1).astype(BF16)


def _ssd_kernel(*refs, nseq):
    st_ref = refs[-1]
    first = pl.program_id(1) == 0

    @pl.when(first)
    def _():
        st_ref[...] = jnp.zeros_like(st_ref)

    for seq in range(nseq):
        _ssd_seq(seq, first, *refs)


def _ssd(main, small, nb, ns, cw, cb, dtb, alog, dexp, ng, consts):
    nseq = _nseq(nb)
    spb = ns // ROWS
    main3 = main.reshape(nb, ns, MAIN_WIDTH)
    small3 = small.reshape(nb, ns, SMALL_WIDTH)
    rowmap = lambda blk: (lambda b, c: (b, c, blk))
    const = lambda shp: pl.BlockSpec(shp, lambda b, c: (0,) * len(shp))
    y = pl.pallas_call(
        functools.partial(_ssd_kernel, nseq=nseq),
        grid=(nb // nseq, spb),
        in_specs=[
            pl.BlockSpec((nseq, ROWS, SSD_CONV_DIM), rowmap(XBC_BLK)),
            pl.BlockSpec((nseq, ROWS, SSD_WIDTH), rowmap(Z_BLK)),
            pl.BlockSpec((nseq, ROWS, SMALL_WIDTH), rowmap(0)),
            const((CONV_K, SSD_CONV_DIM)), const((1, SSD_CONV_DIM)),
            const((1, SMALL_WIDTH)), const((1, SMALL_WIDTH)),
            const((1, SSD_WIDTH)), const((1, SSD_WIDTH)),
            const((2 * SMALL_WIDTH, SSD_WIDTH)), const((ROWS, ROWS)),
            const((ROWS, CPS * LANES)), const((ROWS, CPS * LANES)),
            const((CONV_K - 1, ROWS, ROWS)),
        ],
        out_specs=pl.BlockSpec((nseq, ROWS, SSD_WIDTH), rowmap(0)),
        out_shape=jax.ShapeDtypeStruct((nb, ns, SSD_WIDTH), BF16),
        scratch_shapes=[pltpu.VMEM((nseq, TAIL, SSD_CONV_DIM), F32),
                        pltpu.VMEM((nseq, SSD_STATE, SSD_WIDTH), F32)],
        compiler_params=pltpu.CompilerParams(
            dimension_semantics=("arbitrary", "arbitrary"), vmem_limit_bytes=VMEM_LIMIT),
        name="ssd",
    )(main3, main3, small3, cw, cb, dtb, alog, dexp, ng,
      consts["e_ssd2"], consts["ltri"], consts["u0"], consts["u1"], consts["shift"])
    return y.reshape(nb * ns, SSD_WIDTH)


def _aprep_kernel(q_ref, k_ref, v_ref, gq_ref, gk_ref, bd_ref, qo_ref, ko_ref, vo_ref):
    bd = bd_ref[...]
    q = q_ref[...].astype(F32)
    qms = _dot((q * q).astype(BF16), bd)
    qo_ref[...] = (q * lax.rsqrt(qms + EPS) * gq_ref[...] * (ATTN_HEAD_DIM ** -0.5)).astype(BF16)
    pad = pl.program_id(1) == 0

    @pl.when(pad)
    def _():
        ko_ref[...] = jnp.zeros_like(ko_ref)
        vo_ref[...] = jnp.zeros_like(vo_ref)

    @pl.when(jnp.logical_not(pad))
    def _():
        k = k_ref[...].astype(F32)
        kms = _dot((k * k).astype(BF16), bd)
        ko_ref[...] = (k * lax.rsqrt(kms + EPS) * gk_ref[...]).astype(BF16)
        vo_ref[...] = v_ref[...]


def _aprep(main, nb, ns, gq, gk, bd64):
    t = main.shape[0]
    rb = ATTN_PAD
    spb = ns // rb
    src = lambda blk: (lambda b, i: (b * spb + jnp.maximum(i - 1, 0), blk))
    const = lambda shp: pl.BlockSpec(shp, lambda b, i: (0, 0))
    tp = nb * (ns + ATTN_PAD)
    return pl.pallas_call(
        _aprep_kernel,
        grid=(nb, spb + 1),
        in_specs=[
            pl.BlockSpec((rb, ATTN_WIDTH), src(AQ_BLK)),
            pl.BlockSpec((rb, ATTN_WIDTH), src(AK_BLK)),
            pl.BlockSpec((rb, ATTN_WIDTH), src(AV_BLK)),
            const((1, ATTN_WIDTH)), const((1, ATTN_WIDTH)), const((ATTN_WIDTH, ATTN_WIDTH)),
        ],
        out_specs=[
            pl.BlockSpec((rb, ATTN_WIDTH), lambda b, i: (b * spb + jnp.maximum(i - 1, 0), 0)),
            pl.BlockSpec((rb, ATTN_WIDTH), lambda b, i: (b * (spb + 1) + i, 0)),
            pl.BlockSpec((rb, ATTN_WIDTH), lambda b, i: (b * (spb + 1) + i, 0)),
        ],
        out_shape=[
            jax.ShapeDtypeStruct((t, ATTN_WIDTH), BF16),
            jax.ShapeDtypeStruct((tp, ATTN_WIDTH), BF16),
            jax.ShapeDtypeStruct((tp, ATTN_WIDTH), BF16),
        ],
        compiler_params=pltpu.CompilerParams(
            dimension_semantics=("arbitrary", "arbitrary"), vmem_limit_bytes=VMEM_LIMIT),
        name="attn_prep",
    )(main, main, main, gq, gk, bd64)


def _attn_kernel(q_ref, k_ref, v_ref, gate_ref, bias_ref, o_ref, *, nseq):
    qi = pl.program_id(1)
    start = pl.multiple_of(qi * ATTN_TQ, ATTN_TQ)
    col = lax.broadcasted_iota(jnp.int32, (1, ATTN_NK), 1)
    padrow = jnp.where(col < ATTN_PAD - qi * ATTN_TQ, NEG, 0.0)
    lane_lo = lax.broadcasted_iota(jnp.int32, (ATTN_TQ, LANES), 1) < ATTN_HEAD_DIM
    lane_hi = jnp.logical_not(lane_lo)
    zq = jnp.zeros((ATTN_TQ, LANES), BF16)
    tile = lambda x, h: x[:, (h // 2) * LANES:(h // 2 + 1) * LANES]
    units = [(s, h) for s in range(nseq) for h in range(ATTN_HEADS)]
    q = [q_ref[s] for s in range(nseq)]
    kk = [k_ref[s, pl.ds(start, ATTN_NK), :] for s in range(nseq)]
    vv = [v_ref[s, pl.ds(start, ATTN_NK), :] for s in range(nseq)]
    qm = [jnp.where(lane_lo if h % 2 == 0 else lane_hi, tile(q[s], h), zq) for s, h in units]
    sc = [_dot_nt(qm[u], tile(kk[s], h)) + bias_ref[h] + padrow for u, (s, h) in enumerate(units)]
    mx = [jnp.max(x, axis=-1, keepdims=True) for x in sc]
    p = [jnp.exp(x - m) for x, m in zip(sc, mx)]
    den = [jnp.sum(x, axis=-1, keepdims=True) for x in p]
    ov = [_dot(p[u].astype(BF16), tile(vv[s], h)) / den[u] for u, (s, h) in enumerate(units)]
    for s in range(nseq):
        base = s * ATTN_HEADS
        outs = [jnp.where(lane_lo, ov[base + 2 * j], ov[base + 2 * j + 1]) for j in range(ATTN_HEADS // 2)]
        o = jnp.concatenate(outs, axis=1) * _silu(gate_ref[s].astype(F32))
        o_ref[s] = o.astype(BF16)


def _attn(qn, kn, vn, main, nb, ns, bias):
    nseq = _nseq(nb)
    spb = ns // ATTN_TQ
    kv_rows = ns + ATTN_PAD
    blk = lambda rows, cb: pl.BlockSpec((nseq, rows, ATTN_WIDTH), cb)
    y = pl.pallas_call(
        functools.partial(_attn_kernel, nseq=nseq),
        grid=(nb // nseq, spb),
        in_specs=[
            blk(ATTN_TQ, lambda b, i: (b, i, 0)),
            blk(kv_rows, lambda b, i: (b, 0, 0)),
            blk(kv_rows, lambda b, i: (b, 0, 0)),
            blk(ATTN_TQ, lambda b, i: (b, i, AGATE_BLK)),
            pl.BlockSpec((ATTN_HEADS, ATTN_TQ, ATTN_NK), lambda b, i: (0, 0, 0)),
        ],
        out_specs=blk(ATTN_TQ, lambda b, i: (b, i, 0)),
        out_shape=jax.ShapeDtypeStruct((nb, ns, ATTN_WIDTH), BF16),
        compiler_params=pltpu.CompilerParams(
            dimension_semantics=("arbitrary", "arbitrary"), vmem_limit_bytes=VMEM_LIMIT),
        name="attn",
    )(qn.reshape(nb, ns, ATTN_WIDTH), kn.reshape(nb, kv_rows, ATTN_WIDTH),
      vn.reshape(nb, kv_rows, ATTN_WIDTH), main.reshape(nb, ns, MAIN_WIDTH), bias)
    return y.reshape(nb * ns, ATTN_WIDTH)


def _row_stack_masked(y, groups):
    z = jnp.zeros_like(y)
    return jnp.concatenate([jnp.where(g, y, z) for g in groups], axis=0)


def _tri_inv_minus_eye(a_list, eye, blk, grp):
    bf = lambda xs: [x.astype(BF16) for x in xs]
    bd = lambda xs: [_row_stack_masked(x, grp) for x in xs]
    mm = lambda xs, ys: [_dot(x, y) for x, y in zip(xs, ys)]
    add = lambda xs, ys: [x + y for x, y in zip(xs, ys)]
    d = [jnp.where(blk, a, 0.0) for a in a_list]
    o = [a - x for a, x in zip(a_list, d)]
    db = bf(d)
    d2b = bf(mm(db, bd(db)))
    d2bd = bd(d2b)
    d4b = bf(mm(d2b, d2bd))
    d4bd = bd(d4b)
    d8bd = bd(bf(mm(d4b, d4bd)))
    x = [eye - y for y in d]
    x = add(x, mm(bf(x), d2bd))
    x = add(x, mm(bf(x), d4bd))
    td = add(x, mm(bf(x), d8bd))
    tdm = [y - eye for y in td]
    tdmb = bf(tdm)
    n = add(o, mm(tdmb, bd(bf(o))))
    nb = bf(n)
    n2bd = bd(bf(mm(nb, bd(nb))))
    tnm = [m - y for m, y in zip(mm(bf([eye - y for y in n]), n2bd), n)]
    cross = mm(bf(tnm), bd(tdmb))
    return [p + q + r for p, q, r in zip(tdm, tnm, cross)]


def _gdn_prep(seq, first, qkv_ref, sm_ref, cw_ref, dtb_ref, alog_ref, eb2_ref, eg2_ref, eq2_ref,
              ltri_ref, wcat_ref, bd, shift_ref, tail_ref):
    qkv = _conv_silu(qkv_ref[seq], tail_ref, seq, shift_ref, cw_ref, None, first)
    q = qkv[:, 0:DN_WIDTH]
    k = qkv[:, DN_WIDTH:2 * DN_WIDTH]
    v = qkv[:, 2 * DN_WIDTH:3 * DN_WIDTH]
    q = q * lax.rsqrt(_dot((q * q).astype(BF16), bd) + EPS) * (DN_HEAD_DIM ** -0.5)
    k = k * lax.rsqrt(_dot((k * k).astype(BF16), bd) + EPS)

    sm = sm_ref[seq]
    beta_s = jax.nn.sigmoid(sm)
    g_s = -jnp.exp(alog_ref[...]) * _softplus(sm + dtb_ref[...])
    gcum = _dot2l(ltri_ref[...], g_s)
    g_t = g_s.T
    g_rows = jnp.concatenate([g_t[DNA_COL + h:DNA_COL + h + 1, :] for h in range(DN_HEADS)], axis=1)
    grow = _dot2(jnp.broadcast_to(g_rows, (16, DN_HEADS * ROWS)), wcat_ref[...])[0:1, :]
    gcol = _dot2k(gcum, eq2_ref[...])
    beta_x = _dot2k(beta_s, eb2_ref[...])
    gc_x = _dot2k(gcum, eg2_ref[...])
    egc = jnp.exp(gc_x)
    kb = k * beta_x
    return dict(q=q, k=k, kb=kb, vb=v * beta_x, kbg=kb * egc, qd=q * egc, gc_x=gc_x, gcol=gcol, grow=grow)


def _gdn_kernel(qkv_ref, gate_ref, sm_ref, cw_ref, dtb_ref, alog_ref, ng_ref, eb2_ref, eg2_ref,
                eq2_ref, ltri_ref, wcat_ref, bd_ref, shift_ref, o_ref, tail_ref, st_ref,
                *, nseq):
    first = pl.program_id(1) == 0

    @pl.when(first)
    def _():
        st_ref[...] = jnp.zeros_like(st_ref)

    bd = bd_ref[...]
    prep = [_gdn_prep(seq, first, qkv_ref, sm_ref, cw_ref, dtb_ref, alog_ref, eb2_ref, eg2_ref,
                      eq2_ref, ltri_ref, wcat_ref, bd, shift_ref, tail_ref)
            for seq in range(nseq)]

    qw = DN_HEADS * CHUNK
    ri = lax.broadcasted_iota(jnp.int32, (CHUNK, qw), 0)
    cj = lax.broadcasted_iota(jnp.int32, (CHUNK, qw), 1) & (CHUNK - 1)
    tri = ri >= cj
    stri = ri > cj
    rp = lax.broadcasted_iota(jnp.int32, (CHUNK, LANES), 0)
    lp = lax.broadcasted_iota(jnp.int32, (CHUNK, LANES), 1)
    cp = lp & (CHUNK - 1)
    blk = (rp >> 4) == (cp >> 4)
    eye = jnp.where(rp == cp, 1.0, 0.0).astype(F32)
    grp_p = [lp < CHUNK, lp >= CHUNK]
    lw = lax.broadcasted_iota(jnp.int32, (CHUNK, DN_WIDTH), 1)
    grp_w = [(lw >> 7) == h for h in range(DN_HEADS)]
    lpw = lax.broadcasted_iota(jnp.int32, (CHUNK, PAIR_W), 1)
    grp_pw = [lpw < DN_HEAD_DIM, lpw >= DN_HEAD_DIM]

    chunks = [slice(ci * CHUNK, (ci + 1) * CHUNK) for ci in range(CPS)]
    pairs = range(DN_HEADS // 2)
    units, a_list, attn_list = [], [], []
    for seq in range(nseq):
        pr = prep[seq]
        for ci, rs in enumerate(chunks):
            kstack = _row_stack_masked(pr["k"][rs].astype(BF16), grp_w)
            lhs = jnp.concatenate([pr["kb"][rs], pr["q"][rs]], axis=0).astype(BF16)
            kq = _dot_nt(lhs, kstack)
            diff = pr["gcol"][rs] - pr["grow"][:, ci * qw:(ci + 1) * qw]
            dm = jnp.exp(jnp.where(tri, diff, NEG))
            a = jnp.where(stri, kq[0:CHUNK] * dm, 0.0)
            attn = (kq[CHUNK:2 * CHUNK] * dm).astype(BF16)
            for p in pairs:
                units.append((seq, ci, p))
                a_list.append(a[:, p * LANES:(p + 1) * LANES])
                attn_list.append(attn[:, p * LANES:(p + 1) * LANES])
    tm_list = _tri_inv_minus_eye(a_list, eye, blk, grp_p)
    u_list, w_list = [], []
    for (seq, ci, p), tm in zip(units, tm_list):
        rs = chunks[ci]
        ps = slice(p * PAIR_W, (p + 1) * PAIR_W)
        tmb = tm.astype(BF16)
        vb_p = prep[seq]["vb"][rs, ps]
        kbg_p = prep[seq]["kbg"][rs, ps]
        u_list.append(vb_p + _dot(tmb, _row_stack_masked(vb_p.astype(BF16), grp_pw)))
        w_list.append(kbg_p + _dot(tmb, _row_stack_masked(kbg_p.astype(BF16), grp_pw)))
    unit_of = {u: i for i, u in enumerate(units)}

    outs = [[] for _ in range(nseq)]
    for ci, rs in enumerate(chunks):
        for seq in range(nseq):
            pr = prep[seq]
            gx = pr["gc_x"][rs]
            glast = gx[CHUNK - 1:CHUNK, :]
            kd = pr["k"][rs] * jnp.exp(glast - gx)
            sdec = jnp.exp(glast)
            out_p = []
            for p in pairs:
                i = unit_of[(seq, ci, p)]
                wq = []
                for hh in range(2):
                    h = 2 * p + hh
                    th = slice(h * DN_HEAD_DIM, (h + 1) * DN_HEAD_DIM)
                    tp = slice(hh * DN_HEAD_DIM, (hh + 1) * DN_HEAD_DIM)
                    lhs = jnp.concatenate([w_list[i][:, tp], pr["qd"][rs, th]], axis=0).astype(BF16)
                    wq.append(_dot(lhs, st_ref[seq, h].astype(BF16)))
                vn = u_list[i] - jnp.concatenate([x[0:CHUNK] for x in wq], axis=1)
                vnb = vn.astype(BF16)
                out_p.append(jnp.concatenate([x[CHUNK:2 * CHUNK] for x in wq], axis=1)
                             + _dot(attn_list[i], _row_stack_masked(vnb, grp_pw)))
                for hh in range(2):
                    h = 2 * p + hh
                    th = slice(h * DN_HEAD_DIM, (h + 1) * DN_HEAD_DIM)
                    tp = slice(hh * DN_HEAD_DIM, (hh + 1) * DN_HEAD_DIM)
                    st_ref[seq, h] = (st_ref[seq, h] * sdec[:, th]
                                      + _dot(kd[:, th].T.astype(BF16), vnb[:, tp]))
            outs[seq].append(jnp.concatenate(out_p, axis=1))
    for seq in range(nseq):
        o = jnp.concatenate(outs[seq], axis=0)
        ms = _dot((o * o).astype(BF16), bd) * (1.0 / DN_HEAD_DIM)
        o = o * lax.rsqrt(ms + EPS) * ng_ref[...]
        o_ref[seq] = (o * _silu(gate_ref[seq].astype(F32))).astype(BF16)


def _gdn(main, small, nb, ns, cw, dtb, alog, ng, consts):
    nseq = _nseq(nb)
    spb = ns // ROWS
    main3 = main.reshape(nb, ns, MAIN_WIDTH)
    small3 = small.reshape(nb, ns, SMALL_WIDTH)
    rowmap = lambda blk: (lambda b, c: (b, c, blk))
    const = lambda shp: pl.BlockSpec(shp, lambda b, c: (0,) * len(shp))
    qw = DN_HEADS * CHUNK
    y = pl.pallas_call(
        functools.partial(_gdn_kernel, nseq=nseq),
        grid=(nb // nseq, spb),
        in_specs=[
            pl.BlockSpec((nseq, ROWS, DN_CONV_DIM), rowmap(DNQKV_BLK)),
            pl.BlockSpec((nseq, ROWS, DN_WIDTH), rowmap(DNGATE_BLK)),
            pl.BlockSpec((nseq, ROWS, SMALL_WIDTH), rowmap(0)),
            const((CONV_K, DN_CONV_DIM)),
            const((1, SMALL_WIDTH)), const((1, SMALL_WIDTH)), const((1, DN_WIDTH)),
            const((2 * SMALL_WIDTH, DN_WIDTH)), const((2 * SMALL_WIDTH, DN_WIDTH)),
            const((2 * SMALL_WIDTH, qw)),
            const((ROWS, ROWS)), const((DN_HEADS * ROWS, CPS * qw)),
            const((DN_WIDTH, DN_WIDTH)),
            const((CONV_K - 1, ROWS, ROWS)),
        ],
        out_specs=pl.BlockSpec((nseq, ROWS, DN_WIDTH), rowmap(0)),
        out_shape=jax.ShapeDtypeStruct((nb, ns, DN_WIDTH), BF16),
        scratch_shapes=[pltpu.VMEM((nseq, TAIL, DN_CONV_DIM), F32),
                        pltpu.VMEM((nseq, DN_HEADS, DN_HEAD_DIM, DN_HEAD_DIM), F32)],
        compiler_params=pltpu.CompilerParams(
            dimension_semantics=("arbitrary", "arbitrary"), vmem_limit_bytes=VMEM_LIMIT),
        name="gdn",
    )(main3, main3, small3, cw, dtb, alog, ng,
      consts["e_beta2"], consts["e_g2"], consts["e_gq2"], consts["ltri"], consts["wcat"],
      consts["bd128"], consts["shift"])
    return y.reshape(nb * ns, DN_WIDTH)


_HEAD_PERM = np.concatenate([np.arange(0, SSD_HEADS, 2), np.arange(1, SSD_HEADS, 2)])


def _constants():
    r = np.arange(ROWS)
    same_chunk = (r[:, None] // CHUNK) == (r[None, :] // CHUNK)
    ltri = (same_chunk & (r[None, :] <= r[:, None])).astype(np.float32)
    u0 = np.zeros((ROWS, CPS * LANES), np.float32)
    u1 = np.zeros((ROWS, CPS * LANES), np.float32)
    incl = (np.arange(CHUNK)[:, None] <= np.arange(CHUNK)[None, :]).astype(np.float32)
    for c in range(CPS):
        u0[c * CHUNK:(c + 1) * CHUNK, c * LANES:c * LANES + CHUNK] = incl
        u1[c * CHUNK:(c + 1) * CHUNK, c * LANES + CHUNK:(c + 1) * LANES] = incl
    e_ssd = np.zeros((SMALL_WIDTH, SSD_WIDTH), np.float32)
    for col, h in enumerate(_HEAD_PERM):
        e_ssd[col, h * SSD_HEAD_DIM:(h + 1) * SSD_HEAD_DIM] = 1.0
    qw = DN_HEADS * CHUNK
    e_beta = np.zeros((SMALL_WIDTH, DN_WIDTH), np.float32)
    e_g = np.zeros((SMALL_WIDTH, DN_WIDTH), np.float32)
    e_gq = np.zeros((SMALL_WIDTH, qw), np.float32)
    wcat = np.zeros((DN_HEADS * ROWS, CPS * qw), np.float32)
    for h in range(DN_HEADS):
        e_beta[DNB_COL + h, h * DN_HEAD_DIM:(h + 1) * DN_HEAD_DIM] = 1.0
        e_g[DNA_COL + h, h * DN_HEAD_DIM:(h + 1) * DN_HEAD_DIM] = 1.0
        e_gq[DNA_COL + h, h * CHUNK:(h + 1) * CHUNK] = 1.0
        for c in range(CPS):
            wcat[h * ROWS + c * CHUNK:h * ROWS + (c + 1) * CHUNK,
                 c * qw + h * CHUNK:c * qw + (h + 1) * CHUNK] = incl
    lane = np.arange(DN_WIDTH)
    bd128 = ((lane[:, None] // DN_HEAD_DIM) == (lane[None, :] // DN_HEAD_DIM)).astype(np.float32)
    bd64 = ((lane[:, None] // ATTN_HEAD_DIM) == (lane[None, :] // ATTN_HEAD_DIM)).astype(np.float32)
    bd64 = bd64 / ATTN_HEAD_DIM
    shift = np.zeros((CONV_K - 1, ROWS, ROWS), np.float32)
    for tap in range(CONV_K - 1):
        s = CONV_K - 1 - tap
        shift[tap, np.arange(s, ROWS), np.arange(0, ROWS - s)] = 1.0
    twice = lambda m: np.concatenate([m, m], axis=0)
    out = dict(ltri=ltri, u0=u0, u1=u1, e_ssd2=twice(e_ssd), e_beta2=twice(e_beta), e_g2=twice(e_g),
               e_gq2=twice(e_gq), wcat=wcat, bd128=bd128, bd64=bd64, shift=shift)
    return {k: jnp.asarray(v, BF16) for k, v in out.items()}


def _attn_bias(rel_bias):
    r = np.arange(ATTN_TQ)[:, None]
    c = np.arange(ATTN_NK)[None, :]
    qc, kc = r // CHUNK, c // CHUNK
    valid = (kc >= qc) & (kc <= qc + ATTN_BAND_CHUNKS - 1)
    nh = rel_bias.shape[0]
    assert ATTN_TQ - 1 <= ATTN_REL_CLIP
    near = rel_bias[:, ATTN_REL_CLIP - (ATTN_TQ - 1):2 * ATTN_REL_CLIP]
    far = jnp.broadcast_to(rel_bias[:, 2 * ATTN_REL_CLIP:], (nh, ATTN_NK - ATTN_REL_CLIP))
    t = jnp.concatenate([near, far], axis=1)
    span = ATTN_TQ + ATTN_NK - 1
    w = jnp.concatenate([t[:, ::-1], jnp.zeros((nh, 1), rel_bias.dtype)], axis=1)
    skew = jnp.tile(w, (1, ATTN_TQ))[:, :ATTN_TQ * span].reshape(nh, ATTN_TQ, span)
    bias = skew[:, :, ATTN_TQ - 1:ATTN_TQ - 1 + ATTN_NK]
    return jnp.where(valid[None], bias, NEG).astype(F32)


def _pad_row(v, start):
    return jnp.zeros((1, SMALL_WIDTH), F32).at[0, start:start + v.shape[0]].set(v.astype(F32))


def kernel(x, norm_g, w_in, ssd_conv_w, ssd_conv_b, ssd_dt_bias, ssd_a_log, ssd_d, ssd_norm_g,
           attn_q_norm_g, attn_k_norm_g, attn_rel_bias, dn_conv_w, dn_dt_bias, dn_a_log,
           dn_norm_g, w_out):
    nb, ns, _ = x.shape
    depth = w_in.shape[0]
    assert ns % ROWS == 0 and ns % ATTN_PAD == 0
    consts = _constants()
    x2 = x.reshape(nb * ns, D_MODEL)
    for l in range(depth):
        w = w_in[l]
        w_main = jnp.concatenate(
            [w[:, 1024:2560], w[:, 4624:6160], w[:, 0:1024], w[:, 2576:4624], w[:, 6160:6672]],
            axis=1).astype(BF16)
        w_small = jnp.concatenate(
            [w[:, 2560:2576][:, _HEAD_PERM], w[:, 6672:6680],
             jnp.zeros((D_MODEL, SMALL_WIDTH - 24), F32)], axis=1).astype(BF16)
        main, small = _inproj(x2, norm_g[l][None, :], w_main, w_small)

        y_ssd = _ssd(main, small, nb, ns, ssd_conv_w[l], ssd_conv_b[l][None, :],
                     _pad_row(ssd_dt_bias[l][_HEAD_PERM], 0), _pad_row(ssd_a_log[l][_HEAD_PERM], 0),
                     jnp.repeat(ssd_d[l].astype(F32), SSD_HEAD_DIM)[None, :],
                     ssd_norm_g[l][None, :], consts)

        qn, kn, vn = _aprep(main, nb, ns, jnp.tile(attn_q_norm_g[l], ATTN_HEADS)[None, :],
                            jnp.tile(attn_k_norm_g[l], ATTN_HEADS)[None, :], consts["bd64"])
        y_attn = _attn(qn, kn, vn, main, nb, ns, _attn_bias(attn_rel_bias[l]))

        y_dn = _gdn(main, small, nb, ns, dn_conv_w[l], _pad_row(dn_dt_bias[l], DNA_COL),
                    _pad_row(dn_a_log[l], DNA_COL), jnp.tile(dn_norm_g[l], DN_HEADS)[None, :], consts)

        wo = w_out[l].astype(BF16)
        x2 = _outproj(x2, y_ssd, y_attn, y_dn, wo[0:1024], wo[1024:1536], wo[1536:2048])
    return x2.reshape(nb, ns, D_MODEL)
```

```python
import functools

import numpy as np
import jax
import jax.numpy as jnp
from jax import lax
from jax.experimental import pallas as pl
from jax.experimental.pallas import tpu as pltpu

F32 = jnp.float32
BF16 = jnp.bfloat16

D_MODEL = 1024
CHUNK = 64
EPS = 1e-6
CONV_K = 4
SSD_WIDTH = 1024
SSD_HEADS = 16
SSD_HEAD_DIM = 64
SSD_GROUPS = 2
SSD_STATE = 128
SSD_CONV_DIM = 1536
ATTN_WIDTH = 512
ATTN_HEADS = 8
ATTN_HEAD_DIM = 64
ATTN_BAND_CHUNKS = 9
ATTN_REL_CLIP = 128
DN_WIDTH = 512
DN_HEADS = 4
DN_HEAD_DIM = 128
DN_CONV_DIM = 1536
MIX_WIDTH = 2048

MAIN_WIDTH = 6656
XBC_BLK = 0
DNQKV_BLK = 1
Z_BLK = 3
AQ_BLK, AK_BLK, AV_BLK, AGATE_BLK = 8, 9, 10, 11
DNGATE_BLK = 12
SMALL_WIDTH = 128
DNB_COL = 16
DNA_COL = 20

LANES = 128
NEG = -1e30
VMEM_LIMIT = 56 * 1024 * 1024

ROWS = 256
CPS = ROWS // CHUNK
TAIL = 8
ATTN_TQ = 128
ATTN_NK = ATTN_TQ + (ATTN_BAND_CHUNKS - 1) * CHUNK
ATTN_PAD = (ATTN_BAND_CHUNKS - 1) * CHUNK
PAIR_W = 2 * DN_HEAD_DIM


def _dot(a, b):
    return jnp.dot(a, b, preferred_element_type=F32)


def _dot_nt(a, b):
    return lax.dot_general(a, b, (((1,), (1,)), ((), ())), preferred_element_type=F32)


def _split(v):
    hi = v.astype(BF16)
    lo = (v - hi.astype(F32)).astype(BF16)
    return hi, lo


def _dot2(v, m):
    hi, lo = _split(v)
    return _dot(hi, m) + _dot(lo, m)


def _dot2k(v, m2):
    hi, lo = _split(v)
    return _dot(jnp.concatenate([hi, lo], axis=1), m2)


def _dot2l(m, v):
    hi, lo = _split(v)
    return _dot(m, hi) + _dot(m, lo)


def _silu(x):
    return x * jax.nn.sigmoid(x)


def _softplus(x):
    return jnp.maximum(x, 0.0) + jnp.log1p(jnp.exp(-jnp.abs(x)))


def _conv_silu(xb, tail_ref, seq, shift_ref, cw_ref, bias, first):
    rows = xb.shape[0]

    @pl.when(first)
    def _():
        tail_ref[seq] = jnp.zeros(tail_ref.shape[1:], tail_ref.dtype)

    tail = tail_ref[seq]
    xf = xb.astype(F32)
    acc = cw_ref[CONV_K - 1:CONV_K, :] * xf
    row = lax.broadcasted_iota(jnp.int32, tail.shape, 0)
    shifted = _dot(shift_ref[...], xb)
    head = None
    for t in range(CONV_K - 1):
        s = CONV_K - 1 - t
        w_t = cw_ref[t:t + 1, :]
        acc = acc + w_t * shifted[t * rows:(t + 1) * rows]
        corr = w_t * jnp.where(row < s, pltpu.roll(tail, s, 0), 0.0)
        head = corr if head is None else head + corr
    acc = jnp.concatenate([acc[0:TAIL] + head, acc[TAIL:]], axis=0)
    if bias is not None:
        acc = acc + bias
    tail_ref[seq] = xf[rows - TAIL:rows, :]
    return _silu(acc)


def _head_norm(x, width, scale):
    parts = []
    for h in range(x.shape[1] // width):
        xh = x[:, h * width:(h + 1) * width]
        ss = jnp.sum(xh * xh, axis=-1, keepdims=True)
        parts.append(xh * lax.rsqrt(ss * scale + EPS))
    return jnp.concatenate(parts, axis=1)


def _chunk_pairs(lo, hi):
    lane_lo = lax.broadcasted_iota(jnp.int32, (8, LANES), 1) < CHUNK
    tile = lambda x, k: x[:, k * LANES:(k + 1) * LANES]
    swap = lambda x: pltpu.roll(x, CHUNK, 1)
    out = []
    for ci in range(CPS):
        k = ci // 2
        if ci % 2 == 0:
            out.append(jnp.where(lane_lo, tile(lo, k), swap(tile(hi, k))))
        else:
            out.append(jnp.where(lane_lo, swap(tile(lo, k)), tile(hi, k)))
    return out


def _nseq(nb):
    return 2 if nb % 2 == 0 else 1


def _inproj_kernel(x_ref, g_ref, w_ref, ws_ref, main_ref, small_ref, h_ref):
    @pl.when(pl.program_id(1) == 0)
    def _():
        x = x_ref[...]
        ms = jnp.mean(x * x, axis=-1, keepdims=True)
        h = (x * lax.rsqrt(ms + EPS) * g_ref[...]).astype(BF16)
        h_ref[...] = h
        small_ref[...] = _dot(h, ws_ref[...])

    main_ref[...] = _dot(h_ref[...], w_ref[...]).astype(BF16)


def _inproj(x2, g, w_main, w_small):
    t = x2.shape[0]
    tm = min(1024, t)
    tn = MAIN_WIDTH // 2
    return pl.pallas_call(
        _inproj_kernel,
        grid=(t // tm, MAIN_WIDTH // tn),
        in_specs=[
            pl.BlockSpec((tm, D_MODEL), lambda i, j: (i, 0)),
            pl.BlockSpec((1, D_MODEL), lambda i, j: (0, 0)),
            pl.BlockSpec((D_MODEL, tn), lambda i, j: (0, j)),
            pl.BlockSpec((D_MODEL, SMALL_WIDTH), lambda i, j: (0, 0)),
        ],
        out_specs=[
            pl.BlockSpec((tm, tn), lambda i, j: (i, j)),
            pl.BlockSpec((tm, SMALL_WIDTH), lambda i, j: (i, 0)),
        ],
        out_shape=[
            jax.ShapeDtypeStruct((t, MAIN_WIDTH), BF16),
            jax.ShapeDtypeStruct((t, SMALL_WIDTH), F32),
        ],
        scratch_shapes=[pltpu.VMEM((tm, D_MODEL), BF16)],
        compiler_params=pltpu.CompilerParams(
            dimension_semantics=("arbitrary", "arbitrary"), vmem_limit_bytes=VMEM_LIMIT),
        name="inproj",
    )(x2, g, w_main, w_small)


def _outproj_kernel(x_ref, ys_ref, ya_ref, yd_ref, ws_ref, wa_ref, wd_ref, o_ref):
    acc = _dot(ys_ref[...], ws_ref[...])
    acc = acc + _dot(ya_ref[...], wa_ref[...])
    acc = acc + _dot(yd_ref[...], wd_ref[...])
    o_ref[...] = x_ref[...] + acc


def _outproj(x2, y_ssd, y_attn, y_dn, wo_s, wo_a, wo_d):
    t = x2.shape[0]
    tm = min(1024, t)
    row = lambda w: pl.BlockSpec((tm, w), lambda i: (i, 0))
    full = lambda r: pl.BlockSpec((r, D_MODEL), lambda i: (0, 0))
    return pl.pallas_call(
        _outproj_kernel,
        grid=(t // tm,),
        in_specs=[row(D_MODEL), row(SSD_WIDTH), row(ATTN_WIDTH), row(DN_WIDTH),
                  full(SSD_WIDTH), full(ATTN_WIDTH), full(DN_WIDTH)],
        out_specs=row(D_MODEL),
        out_shape=jax.ShapeDtypeStruct((t, D_MODEL), F32),
        compiler_params=pltpu.CompilerParams(
            dimension_semantics=("arbitrary",), vmem_limit_bytes=VMEM_LIMIT),
        name="outproj",
    )(x2, y_ssd, y_attn, y_dn, wo_s, wo_a, wo_d)


def _ssd_prep(seq, first, xbc_ref, sm_ref, cw_ref, cb_ref, dtb_ref, alog_ref, e2_ref, ltri_ref,
              shift_ref, tail_ref):
    xa = _conv_silu(xbc_ref[seq], tail_ref, seq, shift_ref, cw_ref, cb_ref[...], first)
    xs = xa[:, 0:SSD_WIDTH]
    bm = xa[:, SSD_WIDTH:SSD_WIDTH + 256]
    cm = xa[:, SSD_WIDTH + 256:SSD_WIDTH + 512]

    dt = _softplus(sm_ref[seq] + dtb_ref[...])
    d_a = dt * (-jnp.exp(alog_ref[...]))
    acum = _dot2l(ltri_ref[...], d_a)
    acum_t = acum.T
    prow = _chunk_pairs(acum_t[0:8, :], acum_t[8:16, :])

    e2 = e2_ref[...]
    acum_x = _dot2k(acum, e2)
    dt_x = _dot2k(dt, e2)
    return dict(xs=xs, bm=bm, cm=cm, prow=prow, acum_x=acum_x, xdt=xs * dt_x, eacum=jnp.exp(acum_x))


def _ssd_chunk(pr, seq, ci, st_ref, dexp_ref, tri2, lane_lo):
    rs = slice(ci * CHUNK, (ci + 1) * CHUNK)
    ax = pr["acum_x"][rs]
    last = ax[CHUNK - 1:CHUNK, :]
    dec = jnp.exp(last - ax)
    cdec = jnp.exp(last)
    xdt_c = pr["xdt"][rs]
    xdd = (xdt_c * dec).astype(BF16)
    xdt_b = xdt_c.astype(BF16)
    pc = pr["prow"][ci]
    ydiag, yoff = [], []
    for g in range(SSD_GROUPS):
        gs = slice(g * 512, (g + 1) * 512)
        bg = pr["bm"][rs, g * SSD_STATE:(g + 1) * SSD_STATE]
        cg = pr["cm"][rs, g * SSD_STATE:(g + 1) * SSD_STATE].astype(BF16)
        bgt2 = jnp.concatenate([bg, bg], axis=0).T.astype(BF16)
        cb2 = _dot(cg, bgt2)
        st_g = st_ref[seq, :, gs]
        yoff.append(_dot(cg, st_g.astype(BF16)))
        new_g = _dot(bgt2[:, 0:CHUNK], xdd[:, gs])
        st_ref[seq, :, gs] = st_g * cdec[:, gs] + new_g
        for jp in range(4):
            j = g * 4 + jp
            tl = slice(j * LANES, (j + 1) * LANES)
            diff = ax[:, tl] - pc[j:j + 1, :]
            lp = jnp.exp(jnp.where(tri2, diff, NEG))
            mp = (cb2 * lp).astype(BF16)
            xt = xdt_b[:, tl]
            zx = jnp.zeros_like(xt)
            xbd = jnp.concatenate([jnp.where(lane_lo, xt, zx), jnp.where(lane_lo, zx, xt)],
                                  axis=0)
            ydiag.append(_dot(mp, xbd))
    return (jnp.concatenate(ydiag, axis=1) + jnp.concatenate(yoff, axis=1) * pr["eacum"][rs]
            + pr["xs"][rs] * dexp_ref[...])


def _ssd_kernel(xbc_ref, z_ref, sm_ref, cw_ref, cb_ref, dtb_ref, alog_ref, dexp_ref, ng_ref,
                e2_ref, ltri_ref, shift_ref, y_ref, tail_ref, st_ref, *, nseq):
    first = pl.program_id(1) == 0

    @pl.when(first)
    def _():
        st_ref[...] = jnp.zeros_like(st_ref)

    row_i = lax.broadcasted_iota(jnp.int32, (CHUNK, LANES), 0)
    lane = lax.broadcasted_iota(jnp.int32, (CHUNK, LANES), 1)
    lane_lo = lane < CHUNK
    tri2 = row_i >= jnp.where(lane_lo, lane, lane - CHUNK)

    prep = [_ssd_prep(seq, first, xbc_ref, sm_ref, cw_ref, cb_ref, dtb_ref, alog_ref, e2_ref,
                      ltri_ref, shift_ref, tail_ref) for seq in range(nseq)]
    ys = [[] for _ in range(nseq)]
    for ci in range(CPS):
        for seq in range(nseq):
            ys[seq].append(_ssd_chunk(prep[seq], seq, ci, st_ref, dexp_ref, tri2, lane_lo))
    for seq in range(nseq):
        y = jnp.concatenate(ys[seq], axis=0) * _silu(z_ref[seq].astype(F32))
        outs = []
        for g in range(SSD_GROUPS):
            gs = slice(g * 512, (g + 1) * 512)
            yg = y[:, gs]
            ms = jnp.mean(yg * yg, axis=-1, keepdims=True)
            outs.append(yg * lax.rsqrt(ms + EPS) * ng_ref[:, gs])
        y_ref[seq] = jnp.concatenate(outs, axis=1).astype(BF16)


def _ssd(main, small, nb, ns, cw, cb, dtb, alog, dexp, ng, consts):
    nseq = _nseq(nb)
    spb = ns // ROWS
    main3 = main.reshape(nb, ns, MAIN_WIDTH)
    small3 = small.reshape(nb, ns, SMALL_WIDTH)
    rowmap = lambda blk: (lambda b, c: (b, c, blk))
    const = lambda shp: pl.BlockSpec(shp, lambda b, c: (0,) * len(shp))
    y = pl.pallas_call(
        functools.partial(_ssd_kernel, nseq=nseq),
        grid=(nb // nseq, spb),
        in_specs=[
            pl.BlockSpec((nseq, ROWS, SSD_CONV_DIM), rowmap(XBC_BLK)),
            pl.BlockSpec((nseq, ROWS, SSD_WIDTH), rowmap(Z_BLK)),
            pl.BlockSpec((nseq, ROWS, SMALL_WIDTH), rowmap(0)),
            const((CONV_K, SSD_CONV_DIM)), const((1, SSD_CONV_DIM)),
            const((1, SMALL_WIDTH)), const((1, SMALL_WIDTH)),
            const((1, SSD_WIDTH)), const((1, SSD_WIDTH)),
            const((2 * SMALL_WIDTH, SSD_WIDTH)), const((ROWS, ROWS)),
            const(((CONV_K - 1) * ROWS, ROWS)),
        ],
        out_specs=pl.BlockSpec((nseq, ROWS, SSD_WIDTH), rowmap(0)),
        out_shape=jax.ShapeDtypeStruct((nb, ns, SSD_WIDTH), BF16),
        scratch_shapes=[pltpu.VMEM((nseq, TAIL, SSD_CONV_DIM), F32),
                        pltpu.VMEM((nseq, SSD_STATE, SSD_WIDTH), F32)],
        compiler_params=pltpu.CompilerParams(
            dimension_semantics=("arbitrary", "arbitrary"), vmem_limit_bytes=VMEM_LIMIT),
        name="ssd",
    )(main3, main3, small3, cw, cb, dtb, alog, dexp, ng,
      consts["e_ssd2"], consts["ltri"], consts["shift"])
    return y.reshape(nb * ns, SSD_WIDTH)


def _aprep_kernel(q_ref, k_ref, v_ref, gq_ref, gk_ref, bd_ref, qo_ref, ko_ref, vo_ref):
    bd = bd_ref[...]
    q = q_ref[...].astype(F32)
    qms = _dot((q * q).astype(BF16), bd)
    qo_ref[...] = (q * lax.rsqrt(qms + EPS) * gq_ref[...] * (ATTN_HEAD_DIM ** -0.5)).astype(BF16)
    pad = pl.program_id(1) == 0

    @pl.when(pad)
    def _():
        ko_ref[...] = jnp.zeros_like(ko_ref)
        vo_ref[...] = jnp.zeros_like(vo_ref)

    @pl.when(jnp.logical_not(pad))
    def _():
        k = k_ref[...].astype(F32)
        kms = _dot((k * k).astype(BF16), bd)
        ko_ref[...] = (k * lax.rsqrt(kms + EPS) * gk_ref[...]).astype(BF16)
        vo_ref[...] = v_ref[...]


def _aprep(main, nb, ns, gq, gk, bd64):
    t = main.shape[0]
    rb = ATTN_PAD
    spb = ns // rb
    src = lambda blk: (lambda b, i: (b * spb + jnp.maximum(i - 1, 0), blk))
    const = lambda shp: pl.BlockSpec(shp, lambda b, i: (0, 0))
    tp = nb * (ns + ATTN_PAD)
    return pl.pallas_call(
        _aprep_kernel,
        grid=(nb, spb + 1),
        in_specs=[
            pl.BlockSpec((rb, ATTN_WIDTH), src(AQ_BLK)),
            pl.BlockSpec((rb, ATTN_WIDTH), src(AK_BLK)),
            pl.BlockSpec((rb, ATTN_WIDTH), src(AV_BLK)),
            const((1, ATTN_WIDTH)), const((1, ATTN_WIDTH)), const((ATTN_WIDTH, ATTN_WIDTH)),
        ],
        out_specs=[
            pl.BlockSpec((rb, ATTN_WIDTH), lambda b, i: (b * spb + jnp.maximum(i - 1, 0), 0)),
            pl.BlockSpec((rb, ATTN_WIDTH), lambda b, i: (b * (spb + 1) + i, 0)),
            pl.BlockSpec((rb, ATTN_WIDTH), lambda b, i: (b * (spb + 1) + i, 0)),
        ],
        out_shape=[
            jax.ShapeDtypeStruct((t, ATTN_WIDTH), BF16),
            jax.ShapeDtypeStruct((tp, ATTN_WIDTH), BF16),
            jax.ShapeDtypeStruct((tp, ATTN_WIDTH), BF16),
        ],
        compiler_params=pltpu.CompilerParams(
            dimension_semantics=("arbitrary", "arbitrary"), vmem_limit_bytes=VMEM_LIMIT),
        name="attn_prep",
    )(main, main, main, gq, gk, bd64)


def _attn_kernel(q_ref, k_ref, v_ref, gate_ref, bias_ref, o_ref, *, nseq):
    qi = pl.program_id(1)
    start = pl.multiple_of(qi * ATTN_TQ, ATTN_TQ)
    col = lax.broadcasted_iota(jnp.int32, (1, ATTN_NK), 1)
    padrow = jnp.where(col < ATTN_PAD - qi * ATTN_TQ, NEG, 0.0)
    lane_lo = lax.broadcasted_iota(jnp.int32, (ATTN_TQ, LANES), 1) < ATTN_HEAD_DIM
    lane_hi = jnp.logical_not(lane_lo)
    zq = jnp.zeros((ATTN_TQ, LANES), BF16)
    tile = lambda x, h: x[:, (h // 2) * LANES:(h // 2 + 1) * LANES]
    units = [(s, h) for s in range(nseq) for h in range(ATTN_HEADS)]
    q = [q_ref[s] for s in range(nseq)]
    kk = [k_ref[s, pl.ds(start, ATTN_NK), :] for s in range(nseq)]
    vv = [v_ref[s, pl.ds(start, ATTN_NK), :] for s in range(nseq)]
    qm = [jnp.where(lane_lo if h % 2 == 0 else lane_hi, tile(q[s], h), zq) for s, h in units]
    sc = [_dot_nt(qm[u], tile(kk[s], h)) + bias_ref[h] + padrow for u, (s, h) in enumerate(units)]
    mx = [jnp.max(x, axis=-1, keepdims=True) for x in sc]
    p = [jnp.exp(x - m) for x, m in zip(sc, mx)]
    den = [jnp.sum(x, axis=-1, keepdims=True) for x in p]
    ov = [_dot(p[u].astype(BF16), tile(vv[s], h)) / den[u] for u, (s, h) in enumerate(units)]
    for s in range(nseq):
        base = s * ATTN_HEADS
        outs = [jnp.where(lane_lo, ov[base + 2 * j], ov[base + 2 * j + 1]) for j in range(ATTN_HEADS // 2)]
        o = jnp.concatenate(outs, axis=1) * _silu(gate_ref[s].astype(F32))
        o_ref[s] = o.astype(BF16)


def _attn(qn, kn, vn, main, nb, ns, bias):
    nseq = _nseq(nb)
    spb = ns // ATTN_TQ
    kv_rows = ns + ATTN_PAD
    blk = lambda rows, cb: pl.BlockSpec((nseq, rows, ATTN_WIDTH), cb)
    y = pl.pallas_call(
        functools.partial(_attn_kernel, nseq=nseq),
        grid=(nb // nseq, spb),
        in_specs=[
            blk(ATTN_TQ, lambda b, i: (b, i, 0)),
            blk(kv_rows, lambda b, i: (b, 0, 0)),
            blk(kv_rows, lambda b, i: (b, 0, 0)),
            blk(ATTN_TQ, lambda b, i: (b, i, AGATE_BLK)),
            pl.BlockSpec((ATTN_HEADS, ATTN_TQ, ATTN_NK), lambda b, i: (0, 0, 0)),
        ],
        out_specs=blk(ATTN_TQ, lambda b, i: (b, i, 0)),
        out_shape=jax.ShapeDtypeStruct((nb, ns, ATTN_WIDTH), BF16),
        compiler_params=pltpu.CompilerParams(
            dimension_semantics=("arbitrary", "arbitrary"), vmem_limit_bytes=VMEM_LIMIT),
        name="attn",
    )(qn.reshape(nb, ns, ATTN_WIDTH), kn.reshape(nb, kv_rows, ATTN_WIDTH),
      vn.reshape(nb, kv_rows, ATTN_WIDTH), main.reshape(nb, ns, MAIN_WIDTH), bias)
    return y.reshape(nb * ns, ATTN_WIDTH)


def _row_stack_masked(y, groups):
    z = jnp.zeros_like(y)
    return jnp.concatenate([jnp.where(g, y, z) for g in groups], axis=0)


def _tri_inv_minus_eye(a_list, eye, blk, grp):
    bf = lambda xs: [x.astype(BF16) for x in xs]
    bd = lambda xs: [_row_stack_masked(x, grp) for x in xs]
    mm = lambda xs, ys: [_dot(x, y) for x, y in zip(xs, ys)]
    add = lambda xs, ys: [x + y for x, y in zip(xs, ys)]
    d = [jnp.where(blk, a, 0.0) for a in a_list]
    o = [a - x for a, x in zip(a_list, d)]
    db = bf(d)
    d2b = bf(mm(db, bd(db)))
    d2bd = bd(d2b)
    d4b = bf(mm(d2b, d2bd))
    d4bd = bd(d4b)
    d8bd = bd(bf(mm(d4b, d4bd)))
    x = [eye - y for y in d]
    x = add(x, mm(bf(x), d2bd))
    x = add(x, mm(bf(x), d4bd))
    td = add(x, mm(bf(x), d8bd))
    tdm = [y - eye for y in td]
    tdmb = bf(tdm)
    n = add(o, mm(tdmb, bd(bf(o))))
    nb = bf(n)
    n2bd = bd(bf(mm(nb, bd(nb))))
    tnm = [m - y for m, y in zip(mm(bf([eye - y for y in n]), n2bd), n)]
    cross = mm(bf(tnm), bd(tdmb))
    return [p + q + r for p, q, r in zip(tdm, tnm, cross)]


def _gdn_prep(seq, first, qkv_ref, sm_ref, cw_ref, dtb_ref, alog_ref, eb2_ref, eg2_ref, eq2_ref,
              ltri_ref, shift_ref, tail_ref):
    qkv = _conv_silu(qkv_ref[seq], tail_ref, seq, shift_ref, cw_ref, None, first)
    q = qkv[:, 0:DN_WIDTH]
    k = qkv[:, DN_WIDTH:2 * DN_WIDTH]
    v = qkv[:, 2 * DN_WIDTH:3 * DN_WIDTH]
    q = _head_norm(q, DN_HEAD_DIM, 1.0) * (DN_HEAD_DIM ** -0.5)
    k = _head_norm(k, DN_HEAD_DIM, 1.0)

    sm = sm_ref[seq]
    beta_s = jax.nn.sigmoid(sm)
    g_s = -jnp.exp(alog_ref[...]) * _softplus(sm + dtb_ref[...])
    gcum = _dot2l(ltri_ref[...], g_s)
    g_t = gcum.T[16:24, :]
    grow = _chunk_pairs(g_t, pltpu.roll(g_t, 7, 0))
    gcol = _dot2k(gcum, eq2_ref[...])
    beta_x = _dot2k(beta_s, eb2_ref[...])
    gc_x = _dot2k(gcum, eg2_ref[...])
    egc = jnp.exp(gc_x)
    kb = k * beta_x
    return dict(q=q, k=k, kb=kb, vb=v * beta_x, kbg=kb * egc, qd=q * egc, gc_x=gc_x, gcol=gcol, grow=grow)


def _gdn_kernel(qkv_ref, gate_ref, sm_ref, cw_ref, dtb_ref, alog_ref, ng_ref, eb2_ref, eg2_ref,
                eq2_ref, ltri_ref, shift_ref, o_ref, tail_ref, st_ref,
                *, nseq):
    first = pl.program_id(1) == 0

    @pl.when(first)
    def _():
        st_ref[...] = jnp.zeros_like(st_ref)

    prep = [_gdn_prep(seq, first, qkv_ref, sm_ref, cw_ref, dtb_ref, alog_ref, eb2_ref, eg2_ref,
                      eq2_ref, ltri_ref, shift_ref, tail_ref)
            for seq in range(nseq)]

    qw = DN_HEADS * CHUNK
    rp = lax.broadcasted_iota(jnp.int32, (CHUNK, LANES), 0)
    lp = lax.broadcasted_iota(jnp.int32, (CHUNK, LANES), 1)
    cp = lp & (CHUNK - 1)
    tri = rp >= cp
    stri = rp > cp
    blk = (rp >> 4) == (cp >> 4)
    eye = jnp.where(rp == cp, 1.0, 0.0).astype(F32)
    grp_p = [lp < CHUNK, lp >= CHUNK]
    lpw = lax.broadcasted_iota(jnp.int32, (CHUNK, PAIR_W), 1)
    grp_pw = [lpw < DN_HEAD_DIM, lpw >= DN_HEAD_DIM]

    chunks = [slice(ci * CHUNK, (ci + 1) * CHUNK) for ci in range(CPS)]
    pairs = range(DN_HEADS // 2)
    units, a_list, attn_list = [], [], []
    for seq in range(nseq):
        pr = prep[seq]
        for ci, rs in enumerate(chunks):
            for p in pairs:
                ps = slice(p * PAIR_W, (p + 1) * PAIR_W)
                kstack = _row_stack_masked(pr["k"][rs, ps].astype(BF16), grp_pw)
                lhs = jnp.concatenate([pr["kb"][rs, ps], pr["q"][rs, ps]], axis=0).astype(BF16)
                kq = _dot_nt(lhs, kstack)
                g_row = pr["grow"][ci][DNA_COL - 16 + 2 * p:DNA_COL - 15 + 2 * p, :]
                diff = pr["gcol"][rs, p * LANES:(p + 1) * LANES] - g_row
                dm = jnp.exp(jnp.where(tri, diff, NEG))
                units.append((seq, ci, p))
                a_list.append(jnp.where(stri, kq[0:CHUNK] * dm, 0.0))
                attn_list.append((kq[CHUNK:2 * CHUNK] * dm).astype(BF16))
    tm_list = _tri_inv_minus_eye(a_list, eye, blk, grp_p)
    u_list, w_list = [], []
    for (seq, ci, p), tm in zip(units, tm_list):
        rs = chunks[ci]
        ps = slice(p * PAIR_W, (p + 1) * PAIR_W)
        tmb = tm.astype(BF16)
        vb_p = prep[seq]["vb"][rs, ps]
        kbg_p = prep[seq]["kbg"][rs, ps]
        u_list.append(vb_p + _dot(tmb, _row_stack_masked(vb_p.astype(BF16), grp_pw)))
        w_list.append(kbg_p + _dot(tmb, _row_stack_masked(kbg_p.astype(BF16), grp_pw)))
    unit_of = {u: i for i, u in enumerate(units)}

    outs = [[] for _ in range(nseq)]
    for ci, rs in enumerate(chunks):
        for seq in range(nseq):
            pr = prep[seq]
            gx = pr["gc_x"][rs]
            glast = gx[CHUNK - 1:CHUNK, :]
            kd = pr["k"][rs] * jnp.exp(glast - gx)
            sdec = jnp.exp(glast)
            out_p = []
            for p in pairs:
                i = unit_of[(seq, ci, p)]
                wq = []
                for hh in range(2):
                    h = 2 * p + hh
                    th = slice(h * DN_HEAD_DIM, (h + 1) * DN_HEAD_DIM)
                    tp = slice(hh * DN_HEAD_DIM, (hh + 1) * DN_HEAD_DIM)
                    lhs = jnp.concatenate([w_list[i][:, tp], pr["qd"][rs, th]], axis=0).astype(BF16)
                    wq.append(_dot(lhs, st_ref[seq, h].astype(BF16)))
                vn = u_list[i] - jnp.concatenate([x[0:CHUNK] for x in wq], axis=1)
                vnb = vn.astype(BF16)
                ha, hb = 2 * p, 2 * p + 1
                kd_t = jnp.concatenate([kd[:, ha * DN_HEAD_DIM:(ha + 1) * DN_HEAD_DIM],
                                        kd[:, hb * DN_HEAD_DIM:(hb + 1) * DN_HEAD_DIM]], axis=0).T
                lhs = jnp.concatenate([attn_list[i], kd_t.astype(BF16)], axis=0)
                res = _dot(lhs, _row_stack_masked(vnb, grp_pw))
                out_p.append(jnp.concatenate([x[CHUNK:2 * CHUNK] for x in wq], axis=1) + res[0:CHUNK])
                for hh in range(2):
                    h = 2 * p + hh
                    th = slice(h * DN_HEAD_DIM, (h + 1) * DN_HEAD_DIM)
                    tp = slice(hh * DN_HEAD_DIM, (hh + 1) * DN_HEAD_DIM)
                    st_ref[seq, h] = st_ref[seq, h] * sdec[:, th] + res[CHUNK:CHUNK + DN_HEAD_DIM, tp]
            outs[seq].append(jnp.concatenate(out_p, axis=1))
    for seq in range(nseq):
        o = jnp.concatenate(outs[seq], axis=0)
        o = _head_norm(o, DN_HEAD_DIM, 1.0 / DN_HEAD_DIM) * ng_ref[...]
        o_ref[seq] = (o * _silu(gate_ref[seq].astype(F32))).astype(BF16)


def _gdn(main, small, nb, ns, cw, dtb, alog, ng, consts):
    nseq = _nseq(nb)
    spb = ns // ROWS
    main3 = main.reshape(nb, ns, MAIN_WIDTH)
    small3 = small.reshape(nb, ns, SMALL_WIDTH)
    rowmap = lambda blk: (lambda b, c: (b, c, blk))
    const = lambda shp: pl.BlockSpec(shp, lambda b, c: (0,) * len(shp))
    qw = DN_HEADS * CHUNK
    y = pl.pallas_call(
        functools.partial(_gdn_kernel, nseq=nseq),
        grid=(nb // nseq, spb),
        in_specs=[
            pl.BlockSpec((nseq, ROWS, DN_CONV_DIM), rowmap(DNQKV_BLK)),
            pl.BlockSpec((nseq, ROWS, DN_WIDTH), rowmap(DNGATE_BLK)),
            pl.BlockSpec((nseq, ROWS, SMALL_WIDTH), rowmap(0)),
            const((CONV_K, DN_CONV_DIM)),
            const((1, SMALL_WIDTH)), const((1, SMALL_WIDTH)), const((1, DN_WIDTH)),
            const((2 * SMALL_WIDTH, DN_WIDTH)), const((2 * SMALL_WIDTH, DN_WIDTH)),
            const((2 * SMALL_WIDTH, qw)),
            const((ROWS, ROWS)),
            const(((CONV_K - 1) * ROWS, ROWS)),
        ],
        out_specs=pl.BlockSpec((nseq, ROWS, DN_WIDTH), rowmap(0)),
        out_shape=jax.ShapeDtypeStruct((nb, ns, DN_WIDTH), BF16),
        scratch_shapes=[pltpu.VMEM((nseq, TAIL, DN_CONV_DIM), F32),
                        pltpu.VMEM((nseq, DN_HEADS, DN_HEAD_DIM, DN_HEAD_DIM), F32)],
        compiler_params=pltpu.CompilerParams(
            dimension_semantics=("arbitrary", "arbitrary"), vmem_limit_bytes=VMEM_LIMIT),
        name="gdn",
    )(main3, main3, small3, cw, dtb, alog, ng,
      consts["e_beta2"], consts["e_g2"], consts["e_gq2"], consts["ltri"],
      consts["shift"])
    return y.reshape(nb * ns, DN_WIDTH)


_HEAD_PERM = np.concatenate([np.arange(0, SSD_HEADS, 2), np.arange(1, SSD_HEADS, 2)])


def _constants():
    r = np.arange(ROWS)
    same_chunk = (r[:, None] // CHUNK) == (r[None, :] // CHUNK)
    ltri = (same_chunk & (r[None, :] <= r[:, None])).astype(np.float32)
    e_ssd = np.zeros((SMALL_WIDTH, SSD_WIDTH), np.float32)
    for col, h in enumerate(_HEAD_PERM):
        e_ssd[col, h * SSD_HEAD_DIM:(h + 1) * SSD_HEAD_DIM] = 1.0
    qw = DN_HEADS * CHUNK
    e_beta = np.zeros((SMALL_WIDTH, DN_WIDTH), np.float32)
    e_g = np.zeros((SMALL_WIDTH, DN_WIDTH), np.float32)
    e_gq = np.zeros((SMALL_WIDTH, qw), np.float32)
    for h in range(DN_HEADS):
        e_beta[DNB_COL + h, h * DN_HEAD_DIM:(h + 1) * DN_HEAD_DIM] = 1.0
        e_g[DNA_COL + h, h * DN_HEAD_DIM:(h + 1) * DN_HEAD_DIM] = 1.0
        e_gq[DNA_COL + h, h * CHUNK:(h + 1) * CHUNK] = 1.0
    lane = np.arange(ATTN_WIDTH)
    bd64 = ((lane[:, None] // ATTN_HEAD_DIM) == (lane[None, :] // ATTN_HEAD_DIM)).astype(np.float32)
    bd64 = bd64 / ATTN_HEAD_DIM
    shift = np.zeros(((CONV_K - 1) * ROWS, ROWS), np.float32)
    for tap in range(CONV_K - 1):
        s = CONV_K - 1 - tap
        shift[tap * ROWS + np.arange(s, ROWS), np.arange(0, ROWS - s)] = 1.0
    twice = lambda m: np.concatenate([m, m], axis=0)
    out = dict(ltri=ltri, e_ssd2=twice(e_ssd), e_beta2=twice(e_beta), e_g2=twice(e_g),
               e_gq2=twice(e_gq), bd64=bd64, shift=shift)
    return {k: jnp.asarray(v, BF16) for k, v in out.items()}


def _attn_bias(rel_bias):
    r = np.arange(ATTN_TQ)[:, None]
    c = np.arange(ATTN_NK)[None, :]
    qc, kc = r // CHUNK, c // CHUNK
    valid = (kc >= qc) & (kc <= qc + ATTN_BAND_CHUNKS - 1)
    nh = rel_bias.shape[0]
    assert ATTN_TQ - 1 <= ATTN_REL_CLIP
    near = rel_bias[:, ATTN_REL_CLIP - (ATTN_TQ - 1):2 * ATTN_REL_CLIP]
    far = jnp.broadcast_to(rel_bias[:, 2 * ATTN_REL_CLIP:], (nh, ATTN_NK - ATTN_REL_CLIP))
    t = jnp.concatenate([near, far], axis=1)
    span = ATTN_TQ + ATTN_NK - 1
    w = jnp.concatenate([t[:, ::-1], jnp.zeros((nh, 1), rel_bias.dtype)], axis=1)
    skew = jnp.tile(w, (1, ATTN_TQ))[:, :ATTN_TQ * span].reshape(nh, ATTN_TQ, span)
    bias = skew[:, :, ATTN_TQ - 1:ATTN_TQ - 1 + ATTN_NK]
    return jnp.where(valid[None], bias, NEG).astype(F32)


def _pad_row(v, start):
    return jnp.zeros((1, SMALL_WIDTH), F32).at[0, start:start + v.shape[0]].set(v.astype(F32))


def kernel(x, norm_g, w_in, ssd_conv_w, ssd_conv_b, ssd_dt_bias, ssd_a_log, ssd_d, ssd_norm_g,
           attn_q_norm_g, attn_k_norm_g, attn_rel_bias, dn_conv_w, dn_dt_bias, dn_a_log,
           dn_norm_g, w_out):
    nb, ns, _ = x.shape
    depth = w_in.shape[0]
    assert ns % ROWS == 0 and ns % ATTN_PAD == 0
    consts = _constants()
    x2 = x.reshape(nb * ns, D_MODEL)
    for l in range(depth):
        w = w_in[l]
        w_main = jnp.concatenate(
            [w[:, 1024:2560], w[:, 4624:6160], w[:, 0:1024], w[:, 2576:4624], w[:, 6160:6672]],
            axis=1).astype(BF16)
        w_small = jnp.concatenate(
            [w[:, 2560:2576][:, _HEAD_PERM], w[:, 6672:6680],
             jnp.zeros((D_MODEL, SMALL_WIDTH - 24), F32)], axis=1).astype(BF16)
        main, small = _inproj(x2, norm_g[l][None, :], w_main, w_small)

        y_ssd = _ssd(main, small, nb, ns, ssd_conv_w[l], ssd_conv_b[l][None, :],
                     _pad_row(ssd_dt_bias[l][_HEAD_PERM], 0), _pad_row(ssd_a_log[l][_HEAD_PERM], 0),
                     jnp.repeat(ssd_d[l].astype(F32), SSD_HEAD_DIM)[None, :],
                     ssd_norm_g[l][None, :], consts)

        qn, kn, vn = _aprep(main, nb, ns, jnp.tile(attn_q_norm_g[l], ATTN_HEADS)[None, :],
                            jnp.tile(attn_k_norm_g[l], ATTN_HEADS)[None, :], consts["bd64"])
        y_attn = _attn(qn, kn, vn, main, nb, ns, _attn_bias(attn_rel_bias[l]))

        y_dn = _gdn(main, small, nb, ns, dn_conv_w[l], _pad_row(dn_dt_bias[l], DNA_COL),
                    _pad_row(dn_a_log[l], DNA_COL), jnp.tile(dn_norm_g[l], DN_HEADS)[None, :], consts)

        wo = w_out[l].astype(BF16)
        x2 = _outproj(x2, y_ssd, y_attn, y_dn, wo[0:1024], wo[1024:1536], wo[1536:2048])
    return x2.reshape(nb, ns, D_MODEL)
```

```python
import functools

import numpy as np
import jax
import jax.numpy as jnp
from jax import lax
from jax.experimental import pallas as pl
from jax.experimental.pallas import tpu as pltpu

F32 = jnp.float32
BF16 = jnp.bfloat16

D_MODEL = 1024
CHUNK = 64
EPS = 1e-6
CONV_K = 4
SSD_WIDTH = 1024
SSD_HEADS = 16
SSD_HEAD_DIM = 64
SSD_GROUPS = 2
SSD_STATE = 128
SSD_CONV_DIM = 1536
ATTN_WIDTH = 512
ATTN_HEADS = 8
ATTN_HEAD_DIM = 64
ATTN_BAND_CHUNKS = 9
ATTN_REL_CLIP = 128
DN_WIDTH = 512
DN_HEADS = 4
DN_HEAD_DIM = 128
DN_CONV_DIM = 1536
MIX_WIDTH = 2048

MAIN_WIDTH = 6656
XBC_BLK = 0
DNQKV_BLK = 1
Z_BLK = 3
AQ_BLK, AK_BLK, AV_BLK, AGATE_BLK = 8, 9, 10, 11
DNGATE_BLK = 12
SMALL_WIDTH = 128
DNB_COL = 16
DNA_COL = 20

LANES = 128
NEG = -1e30
VMEM_LIMIT = 56 * 1024 * 1024

ROWS = 256
CPS = ROWS // CHUNK
TAIL = 8
ATTN_TQ = 128
ATTN_NK = ATTN_TQ + (ATTN_BAND_CHUNKS - 1) * CHUNK
ATTN_PAD = (ATTN_BAND_CHUNKS - 1) * CHUNK
PAIR_W = 2 * DN_HEAD_DIM
GDN_NSEQ = 2


def _dot(a, b):
    return jnp.dot(a, b, preferred_element_type=F32)


def _dot_nt(a, b):
    return lax.dot_general(a, b, (((1,), (1,)), ((), ())), preferred_element_type=F32)


def _split(v):
    hi = v.astype(BF16)
    lo = (v - hi.astype(F32)).astype(BF16)
    return hi, lo


def _dot2(v, m):
    hi, lo = _split(v)
    return _dot(hi, m) + _dot(lo, m)


def _dot2k(v, m2):
    hi, lo = _split(v)
    return _dot(jnp.concatenate([hi, lo], axis=1), m2)


def _dot2l(m, v):
    hi, lo = _split(v)
    return _dot(m, hi) + _dot(m, lo)


def _silu(x):
    return x * jax.nn.sigmoid(x)


def _softplus(x):
    return jnp.maximum(x, 0.0) + jnp.log1p(jnp.exp(-jnp.abs(x)))


def _conv_silu(xb, tail_ref, seq, shift_ref, cw_ref, bias, first):
    rows = xb.shape[0]

    @pl.when(first)
    def _():
        tail_ref[seq] = jnp.zeros(tail_ref.shape[1:], tail_ref.dtype)

    tail = tail_ref[seq]
    xf = xb.astype(F32)
    acc = cw_ref[CONV_K - 1:CONV_K, :] * xf
    row = lax.broadcasted_iota(jnp.int32, tail.shape, 0)
    shifted = _dot(shift_ref[...], xb)
    head = None
    for t in range(CONV_K - 1):
        s = CONV_K - 1 - t
        w_t = cw_ref[t:t + 1, :]
        acc = acc + w_t * shifted[t * rows:(t + 1) * rows]
        corr = w_t * jnp.where(row < s, pltpu.roll(tail, s, 0), 0.0)
        head = corr if head is None else head + corr
    acc = jnp.concatenate([acc[0:TAIL] + head, acc[TAIL:]], axis=0)
    if bias is not None:
        acc = acc + bias
    tail_ref[seq] = xf[rows - TAIL:rows, :]
    return _silu(acc)


def _head_norm(x, width, scale):
    parts = []
    for h in range(x.shape[1] // width):
        xh = x[:, h * width:(h + 1) * width]
        ss = jnp.sum(xh * xh, axis=-1, keepdims=True)
        parts.append(xh * lax.rsqrt(ss * scale + EPS))
    return jnp.concatenate(parts, axis=1)


def _chunk_pairs(lo, hi):
    lane_lo = lax.broadcasted_iota(jnp.int32, (8, LANES), 1) < CHUNK
    tile = lambda x, k: x[:, k * LANES:(k + 1) * LANES]
    swap = lambda x: pltpu.roll(x, CHUNK, 1)
    out = []
    for ci in range(CPS):
        k = ci // 2
        if ci % 2 == 0:
            out.append(jnp.where(lane_lo, tile(lo, k), swap(tile(hi, k))))
        else:
            out.append(jnp.where(lane_lo, swap(tile(lo, k)), tile(hi, k)))
    return out


def _nseq(nb, most=2):
    n = most
    while nb % n:
        n //= 2
    return n


INPROJ_TN = MAIN_WIDTH // 2
ATTN_COL0 = AQ_BLK * ATTN_WIDTH - INPROJ_TN


def _inproj_kernel(x_ref, g_ref, w_ref, ws_ref, gq_ref, gk_ref, bd_ref,
                   main_ref, small_ref, qn_ref, kn_ref, vn_ref, h_ref):
    j = pl.program_id(1)

    @pl.when(j == 0)
    def _():
        x = x_ref[...]
        ms = jnp.mean(x * x, axis=-1, keepdims=True)
        h = (x * lax.rsqrt(ms + EPS) * g_ref[...]).astype(BF16)
        h_ref[...] = h
        small_ref[...] = _dot(h, ws_ref[...])

    acc = _dot(h_ref[...], w_ref[...])
    main_ref[...] = acc.astype(BF16)

    @pl.when(j == 1)
    def _():
        bd = bd_ref[...]
        q = acc[:, ATTN_COL0:ATTN_COL0 + ATTN_WIDTH]
        k = acc[:, ATTN_COL0 + ATTN_WIDTH:ATTN_COL0 + 2 * ATTN_WIDTH]
        qms = _dot((q * q).astype(BF16), bd)
        kms = _dot((k * k).astype(BF16), bd)
        qn_ref[...] = (q * lax.rsqrt(qms + EPS) * gq_ref[...] * (ATTN_HEAD_DIM ** -0.5)).astype(BF16)
        kn_ref[...] = (k * lax.rsqrt(kms + EPS) * gk_ref[...]).astype(BF16)
        vn_ref[...] = acc[:, ATTN_COL0 + 2 * ATTN_WIDTH:ATTN_COL0 + 3 * ATTN_WIDTH].astype(BF16)


def _inproj(x2, g, w_main, w_small, gq, gk, bd64):
    t = x2.shape[0]
    tm = min(1024, t)
    assert MAIN_WIDTH // INPROJ_TN == 2
    const = lambda shp: pl.BlockSpec(shp, lambda i, j: (0, 0))
    qkv_spec = pl.BlockSpec((tm, ATTN_WIDTH), lambda i, j: (i, 0))
    qkv_shape = jax.ShapeDtypeStruct((t, ATTN_WIDTH), BF16)
    return pl.pallas_call(
        _inproj_kernel,
        grid=(t // tm, 2),
        in_specs=[
            pl.BlockSpec((tm, D_MODEL), lambda i, j: (i, 0)),
            const((1, D_MODEL)),
            pl.BlockSpec((D_MODEL, INPROJ_TN), lambda i, j: (0, j)),
            const((D_MODEL, SMALL_WIDTH)),
            const((1, ATTN_WIDTH)), const((1, ATTN_WIDTH)), const((ATTN_WIDTH, ATTN_WIDTH)),
        ],
        out_specs=[
            pl.BlockSpec((tm, INPROJ_TN), lambda i, j: (i, j)),
            pl.BlockSpec((tm, SMALL_WIDTH), lambda i, j: (i, 0)),
            qkv_spec, qkv_spec, qkv_spec,
        ],
        out_shape=[
            jax.ShapeDtypeStruct((t, MAIN_WIDTH), BF16),
            jax.ShapeDtypeStruct((t, SMALL_WIDTH), F32),
            qkv_shape, qkv_shape, qkv_shape,
        ],
        scratch_shapes=[pltpu.VMEM((tm, D_MODEL), BF16)],
        compiler_params=pltpu.CompilerParams(
            dimension_semantics=("arbitrary", "arbitrary"), vmem_limit_bytes=VMEM_LIMIT),
        name="inproj",
    )(x2, g, w_main, w_small, gq, gk, bd64)


def _outproj_kernel(x_ref, ys_ref, ya_ref, yd_ref, ws_ref, wa_ref, wd_ref, o_ref):
    acc = _dot(ys_ref[...], ws_ref[...])
    acc = acc + _dot(ya_ref[...], wa_ref[...])
    acc = acc + _dot(yd_ref[...], wd_ref[...])
    o_ref[...] = x_ref[...] + acc


def _outproj(x2, y_ssd, y_attn, y_dn, wo_s, wo_a, wo_d):
    t = x2.shape[0]
    tm = min(1024, t)
    row = lambda w: pl.BlockSpec((tm, w), lambda i: (i, 0))
    full = lambda r: pl.BlockSpec((r, D_MODEL), lambda i: (0, 0))
    return pl.pallas_call(
        _outproj_kernel,
        grid=(t // tm,),
        in_specs=[row(D_MODEL), row(SSD_WIDTH), row(ATTN_WIDTH), row(DN_WIDTH),
                  full(SSD_WIDTH), full(ATTN_WIDTH), full(DN_WIDTH)],
        out_specs=row(D_MODEL),
        out_shape=jax.ShapeDtypeStruct((t, D_MODEL), F32),
        compiler_params=pltpu.CompilerParams(
            dimension_semantics=("arbitrary",), vmem_limit_bytes=VMEM_LIMIT),
        name="outproj",
    )(x2, y_ssd, y_attn, y_dn, wo_s, wo_a, wo_d)


def _ssd_prep(seq, first, xbc_ref, sm_ref, cw_ref, cb_ref, dtb_ref, alog_ref, e2_ref, ltri_ref,
              shift_ref, tail_ref):
    xa = _conv_silu(xbc_ref[seq], tail_ref, seq, shift_ref, cw_ref, cb_ref[...], first)
    xs = xa[:, 0:SSD_WIDTH]
    bm = xa[:, SSD_WIDTH:SSD_WIDTH + 256]
    cm = xa[:, SSD_WIDTH + 256:SSD_WIDTH + 512]

    dt = _softplus(sm_ref[seq] + dtb_ref[...])
    d_a = dt * (-jnp.exp(alog_ref[...]))
    acum = _dot2l(ltri_ref[...], d_a)
    acum_t = acum.T
    prow = _chunk_pairs(acum_t[0:8, :], acum_t[8:16, :])

    e2 = e2_ref[...]
    acum_x = _dot2k(acum, e2)
    dt_x = _dot2k(dt, e2)
    return dict(xs=xs, bm=bm, cm=cm, prow=prow, acum_x=acum_x, xdt=xs * dt_x, eacum=jnp.exp(acum_x))


def _ssd_chunk(pr, seq, ci, st_ref, dexp_ref, tri2, lane_lo):
    rs = slice(ci * CHUNK, (ci + 1) * CHUNK)
    ax = pr["acum_x"][rs]
    last = ax[CHUNK - 1:CHUNK, :]
    dec = jnp.exp(last - ax)
    cdec = jnp.exp(last)
    xdt_c = pr["xdt"][rs]
    xdd = (xdt_c * dec).astype(BF16)
    xdt_b = xdt_c.astype(BF16)
    pc = pr["prow"][ci]
    ydiag, yoff = [], []
    for g in range(SSD_GROUPS):
        gs = slice(g * 512, (g + 1) * 512)
        bg = pr["bm"][rs, g * SSD_STATE:(g + 1) * SSD_STATE]
        cg = pr["cm"][rs, g * SSD_STATE:(g + 1) * SSD_STATE].astype(BF16)
        bgt2 = jnp.concatenate([bg, bg], axis=0).T.astype(BF16)
        cb2 = _dot(cg, bgt2)
        st_g = st_ref[seq, :, gs]
        yoff.append(_dot(cg, st_g.astype(BF16)))
        new_g = _dot(bgt2[:, 0:CHUNK], xdd[:, gs])
        st_ref[seq, :, gs] = st_g * cdec[:, gs] + new_g
        for jp in range(4):
            j = g * 4 + jp
            tl = slice(j * LANES, (j + 1) * LANES)
            diff = ax[:, tl] - pc[j:j + 1, :]
            lp = jnp.exp(jnp.where(tri2, diff, NEG))
            mp = (cb2 * lp).astype(BF16)
            xt = xdt_b[:, tl]
            zx = jnp.zeros_like(xt)
            xbd = jnp.concatenate([jnp.where(lane_lo, xt, zx), jnp.where(lane_lo, zx, xt)],
                                  axis=0)
            ydiag.append(_dot(mp, xbd))
    return (jnp.concatenate(ydiag, axis=1) + jnp.concatenate(yoff, axis=1) * pr["eacum"][rs]
            + pr["xs"][rs] * dexp_ref[...])


def _ssd_kernel(xbc_ref, z_ref, sm_ref, cw_ref, cb_ref, dtb_ref, alog_ref, dexp_ref, ng_ref,
                e2_ref, ltri_ref, shift_ref, y_ref, tail_ref, st_ref, *, nseq):
    first = pl.program_id(1) == 0

    @pl.when(first)
    def _():
        st_ref[...] = jnp.zeros_like(st_ref)

    row_i = lax.broadcasted_iota(jnp.int32, (CHUNK, LANES), 0)
    lane = lax.broadcasted_iota(jnp.int32, (CHUNK, LANES), 1)
    lane_lo = lane < CHUNK
    tri2 = row_i >= jnp.where(lane_lo, lane, lane - CHUNK)

    prep = [_ssd_prep(seq, first, xbc_ref, sm_ref, cw_ref, cb_ref, dtb_ref, alog_ref, e2_ref,
                      ltri_ref, shift_ref, tail_ref) for seq in range(nseq)]
    ys = [[] for _ in range(nseq)]
    for ci in range(CPS):
        for seq in range(nseq):
            ys[seq].append(_ssd_chunk(prep[seq], seq, ci, st_ref, dexp_ref, tri2, lane_lo))
    for seq in range(nseq):
        y = jnp.concatenate(ys[seq], axis=0) * _silu(z_ref[seq].astype(F32))
        outs = []
        for g in range(SSD_GROUPS):
            gs = slice(g * 512, (g + 1) * 512)
            yg = y[:, gs]
            ms = jnp.mean(yg * yg, axis=-1, keepdims=True)
            outs.append(yg * lax.rsqrt(ms + EPS) * ng_ref[:, gs])
        y_ref[seq] = jnp.concatenate(outs, axis=1).astype(BF16)


def _ssd(main, small, nb, ns, cw, cb, dtb, alog, dexp, ng, consts):
    nseq = _nseq(nb)
    spb = ns // ROWS
    main3 = main.reshape(nb, ns, MAIN_WIDTH)
    small3 = small.reshape(nb, ns, SMALL_WIDTH)
    rowmap = lambda blk: (lambda b, c: (b, c, blk))
    const = lambda shp: pl.BlockSpec(shp, lambda b, c: (0,) * len(shp))
    y = pl.pallas_call(
        functools.partial(_ssd_kernel, nseq=nseq),
        grid=(nb // nseq, spb),
        in_specs=[
            pl.BlockSpec((nseq, ROWS, SSD_CONV_DIM), rowmap(XBC_BLK)),
            pl.BlockSpec((nseq, ROWS, SSD_WIDTH), rowmap(Z_BLK)),
            pl.BlockSpec((nseq, ROWS, SMALL_WIDTH), rowmap(0)),
            const((CONV_K, SSD_CONV_DIM)), const((1, SSD_CONV_DIM)),
            const((1, SMALL_WIDTH)), const((1, SMALL_WIDTH)),
            const((1, SSD_WIDTH)), const((1, SSD_WIDTH)),
            const((2 * SMALL_WIDTH, SSD_WIDTH)), const((ROWS, ROWS)),
            const(((CONV_K - 1) * ROWS, ROWS)),
        ],
        out_specs=pl.BlockSpec((nseq, ROWS, SSD_WIDTH), rowmap(0)),
        out_shape=jax.ShapeDtypeStruct((nb, ns, SSD_WIDTH), BF16),
        scratch_shapes=[pltpu.VMEM((nseq, TAIL, SSD_CONV_DIM), F32),
                        pltpu.VMEM((nseq, SSD_STATE, SSD_WIDTH), F32)],
        compiler_params=pltpu.CompilerParams(
            dimension_semantics=("arbitrary", "arbitrary"), vmem_limit_bytes=VMEM_LIMIT),
        name="ssd",
    )(main3, main3, small3, cw, cb, dtb, alog, dexp, ng,
      consts["e_ssd2"], consts["ltri"], consts["shift"])
    return y.reshape(nb * ns, SSD_WIDTH)


def _attn_kernel(q_ref, k_ref, v_ref, gate_ref, bias_ref, o_ref, *, nseq):
    qi = pl.program_id(1)
    start = pl.multiple_of(jnp.maximum(qi * ATTN_TQ - ATTN_PAD, 0), ATTN_TQ)
    lane_lo = lax.broadcasted_iota(jnp.int32, (ATTN_TQ, LANES), 1) < ATTN_HEAD_DIM
    lane_hi = jnp.logical_not(lane_lo)
    zq = jnp.zeros((ATTN_TQ, LANES), BF16)
    tile = lambda x, h: x[:, (h // 2) * LANES:(h // 2 + 1) * LANES]
    units = [(s, h) for s in range(nseq) for h in range(ATTN_HEADS)]
    q = [q_ref[s] for s in range(nseq)]
    kk = [k_ref[s, pl.ds(start, ATTN_NK), :] for s in range(nseq)]
    vv = [v_ref[s, pl.ds(start, ATTN_NK), :] for s in range(nseq)]
    qm = [jnp.where(lane_lo if h % 2 == 0 else lane_hi, tile(q[s], h), zq) for s, h in units]
    sc = [_dot_nt(qm[u], tile(kk[s], h)) + bias_ref[0, h] for u, (s, h) in enumerate(units)]
    mx = [jnp.max(x, axis=-1, keepdims=True) for x in sc]
    p = [jnp.exp(x - m) for x, m in zip(sc, mx)]
    den = [jnp.sum(x, axis=-1, keepdims=True) for x in p]
    ov = [_dot(p[u].astype(BF16), tile(vv[s], h)) / den[u] for u, (s, h) in enumerate(units)]
    for s in range(nseq):
        base = s * ATTN_HEADS
        outs = [jnp.where(lane_lo, ov[base + 2 * j], ov[base + 2 * j + 1]) for j in range(ATTN_HEADS // 2)]
        o = jnp.concatenate(outs, axis=1) * _silu(gate_ref[s].astype(F32))
        o_ref[s] = o.astype(BF16)


def _attn(qn, kn, vn, main, nb, ns, bias):
    nseq = _nseq(nb)
    spb = ns // ATTN_TQ
    assert ns >= ATTN_NK
    nvar = bias.shape[0]
    blk = lambda rows, cb: pl.BlockSpec((nseq, rows, ATTN_WIDTH), cb)
    y = pl.pallas_call(
        functools.partial(_attn_kernel, nseq=nseq),
        grid=(nb // nseq, spb),
        in_specs=[
            blk(ATTN_TQ, lambda b, i: (b, i, 0)),
            blk(ns, lambda b, i: (b, 0, 0)),
            blk(ns, lambda b, i: (b, 0, 0)),
            blk(ATTN_TQ, lambda b, i: (b, i, AGATE_BLK)),
            pl.BlockSpec((1, ATTN_HEADS, ATTN_TQ, ATTN_NK),
                         lambda b, i: (jnp.minimum(i, nvar - 1), 0, 0, 0)),
        ],
        out_specs=blk(ATTN_TQ, lambda b, i: (b, i, 0)),
        out_shape=jax.ShapeDtypeStruct((nb, ns, ATTN_WIDTH), BF16),
        compiler_params=pltpu.CompilerParams(
            dimension_semantics=("arbitrary", "arbitrary"), vmem_limit_bytes=VMEM_LIMIT),
        name="attn",
    )(qn.reshape(nb, ns, ATTN_WIDTH), kn.reshape(nb, ns, ATTN_WIDTH),
      vn.reshape(nb, ns, ATTN_WIDTH), main.reshape(nb, ns, MAIN_WIDTH), bias)
    return y.reshape(nb * ns, ATTN_WIDTH)


def _row_stack_masked(y, groups):
    z = jnp.zeros_like(y)
    return jnp.concatenate([jnp.where(g, y, z) for g in groups], axis=0)


def _tri_inv_minus_eye(a_list, eye, blk, grp):
    bf = lambda xs: [x.astype(BF16) for x in xs]
    bd = lambda xs: [_row_stack_masked(x, grp) for x in xs]
    mm = lambda xs, ys: [_dot(x, y) for x, y in zip(xs, ys)]
    add = lambda xs, ys: [x + y for x, y in zip(xs, ys)]
    d = [jnp.where(blk, a, 0.0) for a in a_list]
    o = [a - x for a, x in zip(a_list, d)]
    db = bf(d)
    d2b = bf(mm(db, bd(db)))
    d2bd = bd(d2b)
    d4b = bf(mm(d2b, d2bd))
    d4bd = bd(d4b)
    d8bd = bd(bf(mm(d4b, d4bd)))
    x = [eye - y for y in d]
    x = add(x, mm(bf(x), d2bd))
    x = add(x, mm(bf(x), d4bd))
    td = add(x, mm(bf(x), d8bd))
    tdm = [y - eye for y in td]
    tdmb = bf(tdm)
    n = add(o, mm(tdmb, bd(bf(o))))
    nb = bf(n)
    n2bd = bd(bf(mm(nb, bd(nb))))
    tnm = [m - y for m, y in zip(mm(bf([eye - y for y in n]), n2bd), n)]
    cross = mm(bf(tnm), bd(tdmb))
    return [p + q + r for p, q, r in zip(tdm, tnm, cross)]


def _gdn_prep(seq, first, qkv_ref, sm_ref, cw_ref, dtb_ref, alog_ref, eb2_ref, eg2_ref, eq2_ref,
              ltri_ref, shift_ref, tail_ref):
    qkv = _conv_silu(qkv_ref[seq], tail_ref, seq, shift_ref, cw_ref, None, first)
    q = qkv[:, 0:DN_WIDTH]
    k = qkv[:, DN_WIDTH:2 * DN_WIDTH]
    v = qkv[:, 2 * DN_WIDTH:3 * DN_WIDTH]
    q = _head_norm(q, DN_HEAD_DIM, 1.0) * (DN_HEAD_DIM ** -0.5)
    k = _head_norm(k, DN_HEAD_DIM, 1.0)

    sm = sm_ref[seq]
    beta_s = jax.nn.sigmoid(sm)
    g_s = -jnp.exp(alog_ref[...]) * _softplus(sm + dtb_ref[...])
    gcum = _dot2l(ltri_ref[...], g_s)
    g_t = gcum.T[16:24, :]
    grow = _chunk_pairs(g_t, pltpu.roll(g_t, 7, 0))
    gcol = _dot2k(gcum, eq2_ref[...])
    beta_x = _dot2k(beta_s, eb2_ref[...])
    gc_x = _dot2k(gcum, eg2_ref[...])
    egc = jnp.exp(gc_x)
    kb = k * beta_x
    return dict(q=q, k=k, kb=kb, vb=v * beta_x, kbg=kb * egc, qd=q * egc, gc_x=gc_x, gcol=gcol, grow=grow)


def _gdn_kernel(qkv_ref, gate_ref, sm_ref, cw_ref, dtb_ref, alog_ref, ng_ref, eb2_ref, eg2_ref,
                eq2_ref, ltri_ref, shift_ref, o_ref, tail_ref, st_ref,
                *, nseq):
    first = pl.program_id(1) == 0

    @pl.when(first)
    def _():
        st_ref[...] = jnp.zeros_like(st_ref)

    prep = [_gdn_prep(seq, first, qkv_ref, sm_ref, cw_ref, dtb_ref, alog_ref, eb2_ref, eg2_ref,
                      eq2_ref, ltri_ref, shift_ref, tail_ref)
            for seq in range(nseq)]

    qw = DN_HEADS * CHUNK
    rp = lax.broadcasted_iota(jnp.int32, (CHUNK, LANES), 0)
    lp = lax.broadcasted_iota(jnp.int32, (CHUNK, LANES), 1)
    cp = lp & (CHUNK - 1)
    tri = rp >= cp
    stri = rp > cp
    blk = (rp >> 4) == (cp >> 4)
    eye = jnp.where(rp == cp, 1.0, 0.0).astype(F32)
    grp_p = [lp < CHUNK, lp >= CHUNK]
    lpw = lax.broadcasted_iota(jnp.int32, (CHUNK, PAIR_W), 1)
    grp_pw = [lpw < DN_HEAD_DIM, lpw >= DN_HEAD_DIM]

    chunks = [slice(ci * CHUNK, (ci + 1) * CHUNK) for ci in range(CPS)]
    pairs = range(DN_HEADS // 2)
    units, a_list, attn_list = [], [], []
    for seq in range(nseq):
        pr = prep[seq]
        for ci, rs in enumerate(chunks):
            for p in pairs:
                ps = slice(p * PAIR_W, (p + 1) * PAIR_W)
                kstack = _row_stack_masked(pr["k"][rs, ps].astype(BF16), grp_pw)
                lhs = jnp.concatenate([pr["kb"][rs, ps], pr["q"][rs, ps]], axis=0).astype(BF16)
                kq = _dot_nt(lhs, kstack)
                g_row = pr["grow"][ci][DNA_COL - 16 + 2 * p:DNA_COL - 15 + 2 * p, :]
                diff = pr["gcol"][rs, p * LANES:(p + 1) * LANES] - g_row
                dm = jnp.exp(jnp.where(tri, diff, NEG))
                units.append((seq, ci, p))
                a_list.append(jnp.where(stri, kq[0:CHUNK] * dm, 0.0))
                attn_list.append((kq[CHUNK:2 * CHUNK] * dm).astype(BF16))
    tm_list = _tri_inv_minus_eye(a_list, eye, blk, grp_p)
    u_list, w_list = [], []
    for (seq, ci, p), tm in zip(units, tm_list):
        rs = chunks[ci]
        ps = slice(p * PAIR_W, (p + 1) * PAIR_W)
        tmb = tm.astype(BF16)
        vb_p = prep[seq]["vb"][rs, ps]
        kbg_p = prep[seq]["kbg"][rs, ps]
        u_list.append(vb_p + _dot(tmb, _row_stack_masked(vb_p.astype(BF16), grp_pw)))
        w_list.append(kbg_p + _dot(tmb, _row_stack_masked(kbg_p.astype(BF16), grp_pw)))
    unit_of = {u: i for i, u in enumerate(units)}

    outs = [[] for _ in range(nseq)]
    for ci, rs in enumerate(chunks):
        for seq in range(nseq):
            pr = prep[seq]
            gx = pr["gc_x"][rs]
            glast = gx[CHUNK - 1:CHUNK, :]
            kd = pr["k"][rs] * jnp.exp(glast - gx)
            sdec = jnp.exp(glast)
            out_p = []
            for p in pairs:
                i = unit_of[(seq, ci, p)]
                wq = []
                for hh in range(2):
                    h = 2 * p + hh
                    th = slice(h * DN_HEAD_DIM, (h + 1) * DN_HEAD_DIM)
                    tp = slice(hh * DN_HEAD_DIM, (hh + 1) * DN_HEAD_DIM)
                    lhs = jnp.concatenate([w_list[i][:, tp], pr["qd"][rs, th]], axis=0).astype(BF16)
                    wq.append(_dot(lhs, st_ref[seq, h].astype(BF16)))
                vn = u_list[i] - jnp.concatenate([x[0:CHUNK] for x in wq], axis=1)
                vnb = vn.astype(BF16)
                ha, hb = 2 * p, 2 * p + 1
                kd_t = jnp.concatenate([kd[:, ha * DN_HEAD_DIM:(ha + 1) * DN_HEAD_DIM],
                                        kd[:, hb * DN_HEAD_DIM:(hb + 1) * DN_HEAD_DIM]], axis=0).T
                lhs = jnp.concatenate([attn_list[i], kd_t.astype(BF16)], axis=0)
                res = _dot(lhs, _row_stack_masked(vnb, grp_pw))
                out_p.append(jnp.concatenate([x[CHUNK:2 * CHUNK] for x in wq], axis=1) + res[0:CHUNK])
                for hh in range(2):
                    h = 2 * p + hh
                    th = slice(h * DN_HEAD_DIM, (h + 1) * DN_HEAD_DIM)
                    tp = slice(hh * DN_HEAD_DIM, (hh + 1) * DN_HEAD_DIM)
                    st_ref[seq, h] = st_ref[seq, h] * sdec[:, th] + res[CHUNK:CHUNK + DN_HEAD_DIM, tp]
            outs[seq].append(jnp.concatenate(out_p, axis=1))
    for seq in range(nseq):
        o = jnp.concatenate(outs[seq], axis=0)
        o = _head_norm(o, DN_HEAD_DIM, 1.0 / DN_HEAD_DIM) * ng_ref[...]
        o_ref[seq] = (o * _silu(gate_ref[seq].astype(F32))).astype(BF16)


def _gdn(main, small, nb, ns, cw, dtb, alog, ng, consts):
    nseq = _nseq(nb, GDN_NSEQ)
    spb = ns // ROWS
    main3 = main.reshape(nb, ns, MAIN_WIDTH)
    small3 = small.reshape(nb, ns, SMALL_WIDTH)
    rowmap = lambda blk: (lambda b, c: (b, c, blk))
    const = lambda shp: pl.BlockSpec(shp, lambda b, c: (0,) * len(shp))
    qw = DN_HEADS * CHUNK
    y = pl.pallas_call(
        functools.partial(_gdn_kernel, nseq=nseq),
        grid=(nb // nseq, spb),
        in_specs=[
            pl.BlockSpec((nseq, ROWS, DN_CONV_DIM), rowmap(DNQKV_BLK)),
            pl.BlockSpec((nseq, ROWS, DN_WIDTH), rowmap(DNGATE_BLK)),
            pl.BlockSpec((nseq, ROWS, SMALL_WIDTH), rowmap(0)),
            const((CONV_K, DN_CONV_DIM)),
            const((1, SMALL_WIDTH)), const((1, SMALL_WIDTH)), const((1, DN_WIDTH)),
            const((2 * SMALL_WIDTH, DN_WIDTH)), const((2 * SMALL_WIDTH, DN_WIDTH)),
            const((2 * SMALL_WIDTH, qw)),
            const((ROWS, ROWS)),
            const(((CONV_K - 1) * ROWS, ROWS)),
        ],
        out_specs=pl.BlockSpec((nseq, ROWS, DN_WIDTH), rowmap(0)),
        out_shape=jax.ShapeDtypeStruct((nb, ns, DN_WIDTH), BF16),
        scratch_shapes=[pltpu.VMEM((nseq, TAIL, DN_CONV_DIM), F32),
                        pltpu.VMEM((nseq, DN_HEADS, DN_HEAD_DIM, DN_HEAD_DIM), F32)],
        compiler_params=pltpu.CompilerParams(
            dimension_semantics=("arbitrary", "arbitrary"), vmem_limit_bytes=VMEM_LIMIT),
        name="gdn",
    )(main3, main3, small3, cw, dtb, alog, ng,
      consts["e_beta2"], consts["e_g2"], consts["e_gq2"], consts["ltri"],
      consts["shift"])
    return y.reshape(nb * ns, DN_WIDTH)


_HEAD_PERM = np.concatenate([np.arange(0, SSD_HEADS, 2), np.arange(1, SSD_HEADS, 2)])


def _constants():
    r = np.arange(ROWS)
    same_chunk = (r[:, None] // CHUNK) == (r[None, :] // CHUNK)
    ltri = (same_chunk & (r[None, :] <= r[:, None])).astype(np.float32)
    e_ssd = np.zeros((SMALL_WIDTH, SSD_WIDTH), np.float32)
    for col, h in enumerate(_HEAD_PERM):
        e_ssd[col, h * SSD_HEAD_DIM:(h + 1) * SSD_HEAD_DIM] = 1.0
    qw = DN_HEADS * CHUNK
    e_beta = np.zeros((SMALL_WIDTH, DN_WIDTH), np.float32)
    e_g = np.zeros((SMALL_WIDTH, DN_WIDTH), np.float32)
    e_gq = np.zeros((SMALL_WIDTH, qw), np.float32)
    for h in range(DN_HEADS):
        e_beta[DNB_COL + h, h * DN_HEAD_DIM:(h + 1) * DN_HEAD_DIM] = 1.0
        e_g[DNA_COL + h, h * DN_HEAD_DIM:(h + 1) * DN_HEAD_DIM] = 1.0
        e_gq[DNA_COL + h, h * CHUNK:(h + 1) * CHUNK] = 1.0
    lane = np.arange(ATTN_WIDTH)
    bd64 = ((lane[:, None] // ATTN_HEAD_DIM) == (lane[None, :] // ATTN_HEAD_DIM)).astype(np.float32)
    bd64 = bd64 / ATTN_HEAD_DIM
    shift = np.zeros(((CONV_K - 1) * ROWS, ROWS), np.float32)
    for tap in range(CONV_K - 1):
        s = CONV_K - 1 - tap
        shift[tap * ROWS + np.arange(s, ROWS), np.arange(0, ROWS - s)] = 1.0
    twice = lambda m: np.concatenate([m, m], axis=0)
    out = dict(ltri=ltri, e_ssd2=twice(e_ssd), e_beta2=twice(e_beta), e_g2=twice(e_g),
               e_gq2=twice(e_gq), bd64=bd64, shift=shift)
    return {k: jnp.asarray(v, BF16) for k, v in out.items()}


def _attn_bias(rel_bias):
    r = np.arange(ATTN_TQ)[:, None]
    c = np.arange(ATTN_NK)[None, :]
    qc, kc = r // CHUNK, c // CHUNK
    valid = (kc >= qc) & (kc <= qc + ATTN_BAND_CHUNKS - 1)
    nh = rel_bias.shape[0]
    assert ATTN_TQ - 1 <= ATTN_REL_CLIP
    near = rel_bias[:, ATTN_REL_CLIP - (ATTN_TQ - 1):2 * ATTN_REL_CLIP]
    far = jnp.broadcast_to(rel_bias[:, 2 * ATTN_REL_CLIP:], (nh, ATTN_NK - ATTN_REL_CLIP))
    t = jnp.concatenate([near, far], axis=1)
    span = ATTN_TQ + ATTN_NK - 1
    w = jnp.concatenate([t[:, ::-1], jnp.zeros((nh, 1), rel_bias.dtype)], axis=1)
    skew = jnp.tile(w, (1, ATTN_TQ))[:, :ATTN_TQ * span].reshape(nh, ATTN_TQ, span)
    bias = skew[:, :, ATTN_TQ - 1:ATTN_TQ - 1 + ATTN_NK]
    bias = jnp.where(valid[None], bias, NEG).astype(F32)
    ext = jnp.concatenate([bias, jnp.full((nh, ATTN_TQ, ATTN_PAD), NEG, F32)], axis=2)
    offs = [max(ATTN_PAD - v * ATTN_TQ, 0) for v in range(ATTN_PAD // ATTN_TQ + 1)]
    return jnp.stack([ext[:, :, o:o + ATTN_NK] for o in offs], axis=0)


def _pad_row(v, start):
    return jnp.zeros((1, SMALL_WIDTH), F32).at[0, start:start + v.shape[0]].set(v.astype(F32))


def kernel(x, norm_g, w_in, ssd_conv_w, ssd_conv_b, ssd_dt_bias, ssd_a_log, ssd_d, ssd_norm_g,
           attn_q_norm_g, attn_k_norm_g, attn_rel_bias, dn_conv_w, dn_dt_bias, dn_a_log,
           dn_norm_g, w_out):
    nb, ns, _ = x.shape
    depth = w_in.shape[0]
    assert ns % ROWS == 0 and ns % ATTN_PAD == 0
    consts = _constants()
    x2 = x.reshape(nb * ns, D_MODEL)
    for l in range(depth):
        w = w_in[l]
        w_main = jnp.concatenate(
            [w[:, 1024:2560], w[:, 4624:6160], w[:, 0:1024], w[:, 2576:4624], w[:, 6160:6672]],
            axis=1).astype(BF16)
        w_small = jnp.concatenate(
            [w[:, 2560:2576][:, _HEAD_PERM], w[:, 6672:6680],
             jnp.zeros((D_MODEL, SMALL_WIDTH - 24), F32)], axis=1).astype(BF16)
        main, small, qn, kn, vn = _inproj(
            x2, norm_g[l][None, :], w_main, w_small,
            jnp.tile(attn_q_norm_g[l], ATTN_HEADS)[None, :],
            jnp.tile(attn_k_norm_g[l], ATTN_HEADS)[None, :], consts["bd64"])

        y_ssd = _ssd(main, small, nb, ns, ssd_conv_w[l], ssd_conv_b[l][None, :],
                     _pad_row(ssd_dt_bias[l][_HEAD_PERM], 0), _pad_row(ssd_a_log[l][_HEAD_PERM], 0),
                     jnp.repeat(ssd_d[l].astype(F32), SSD_HEAD_DIM)[None, :],
                     ssd_norm_g[l][None, :], consts)

        y_attn = _attn(qn, kn, vn, main, nb, ns, _attn_bias(attn_rel_bias[l]))

        y_dn = _gdn(main, small, nb, ns, dn_conv_w[l], _pad_row(dn_dt_bias[l], DNA_COL),
                    _pad_row(dn_a_log[l], DNA_COL), jnp.tile(dn_norm_g[l], DN_HEADS)[None, :], consts)

        wo = w_out[l].astype(BF16)
        x2 = _outproj(x2, y_ssd, y_attn, y_dn, wo[0:1024], wo[1024:1536], wo[1536:2048])
    return x2.reshape(nb, ns, D_MODEL)
```

```python
import functools

import numpy as np
import jax
import jax.numpy as jnp
from jax import lax
from jax.experimental import pallas as pl
from jax.experimental.pallas import tpu as pltpu

F32 = jnp.float32
BF16 = jnp.bfloat16

D_MODEL = 1024
CHUNK = 64
EPS = 1e-6
CONV_K = 4
SSD_WIDTH = 1024
SSD_HEADS = 16
SSD_HEAD_DIM = 64
SSD_GROUPS = 2
SSD_STATE = 128
SSD_CONV_DIM = 1536
ATTN_WIDTH = 512
ATTN_HEADS = 8
ATTN_HEAD_DIM = 64
ATTN_BAND_CHUNKS = 9
ATTN_REL_CLIP = 128
DN_WIDTH = 512
DN_HEADS = 4
DN_HEAD_DIM = 128
DN_CONV_DIM = 1536
MIX_WIDTH = 2048

MAIN_WIDTH = 6656
XBC_BLK = 0
DNQKV_BLK = 1
Z_BLK = 3
AQ_BLK, AK_BLK, AV_BLK, AGATE_BLK = 8, 9, 10, 11
DNGATE_BLK = 12
SMALL_WIDTH = 128
DNB_COL = 16
DNA_COL = 20

LANES = 128
NEG = -1e30
VMEM_LIMIT = 56 * 1024 * 1024

ROWS = 256
CPS = ROWS // CHUNK
TAIL = 8
ATTN_TQ = 128
ATTN_NK = ATTN_TQ + (ATTN_BAND_CHUNKS - 1) * CHUNK
ATTN_PAD = (ATTN_BAND_CHUNKS - 1) * CHUNK
PAIR_W = 2 * DN_HEAD_DIM
GDN_NSEQ = 4
SSD_NSEQ = 4


def _dot(a, b):
    return jnp.dot(a, b, preferred_element_type=F32)


def _dot_nt(a, b):
    return lax.dot_general(a, b, (((1,), (1,)), ((), ())), preferred_element_type=F32)


def _split(v):
    hi = v.astype(BF16)
    lo = (v - hi.astype(F32)).astype(BF16)
    return hi, lo


def _dot2(v, m):
    hi, lo = _split(v)
    return _dot(hi, m) + _dot(lo, m)


def _dot2k(v, m2):
    hi, lo = _split(v)
    return _dot(jnp.concatenate([hi, lo], axis=1), m2)


def _dot2l(m, v):
    hi, lo = _split(v)
    return _dot(m, hi) + _dot(m, lo)


def _silu(x):
    return x * jax.nn.sigmoid(x)


def _softplus(x):
    return jnp.maximum(x, 0.0) + jnp.log1p(jnp.exp(-jnp.abs(x)))


def _conv_silu(xb, tail_ref, seq, shift_ref, cw_ref, bias, first):
    rows = xb.shape[0]

    @pl.when(first)
    def _():
        tail_ref[seq] = jnp.zeros(tail_ref.shape[1:], tail_ref.dtype)

    tail = tail_ref[seq]
    xf = xb.astype(F32)
    acc = cw_ref[CONV_K - 1:CONV_K, :] * xf
    row = lax.broadcasted_iota(jnp.int32, tail.shape, 0)
    shifted = _dot(shift_ref[...], xb)
    head = None
    for t in range(CONV_K - 1):
        s = CONV_K - 1 - t
        w_t = cw_ref[t:t + 1, :]
        acc = acc + w_t * shifted[t * rows:(t + 1) * rows]
        corr = w_t * jnp.where(row < s, pltpu.roll(tail, s, 0), 0.0)
        head = corr if head is None else head + corr
    acc = jnp.concatenate([acc[0:TAIL] + head, acc[TAIL:]], axis=0)
    if bias is not None:
        acc = acc + bias
    tail_ref[seq] = xf[rows - TAIL:rows, :]
    return _silu(acc)


def _head_norm(x, width, scale):
    parts = []
    for h in range(x.shape[1] // width):
        xh = x[:, h * width:(h + 1) * width]
        ss = jnp.sum(xh * xh, axis=-1, keepdims=True)
        parts.append(xh * lax.rsqrt(ss * scale + EPS))
    return jnp.concatenate(parts, axis=1)


def _chunk_pairs(lo, hi):
    lane_lo = lax.broadcasted_iota(jnp.int32, (8, LANES), 1) < CHUNK
    tile = lambda x, k: x[:, k * LANES:(k + 1) * LANES]
    swap = lambda x: pltpu.roll(x, CHUNK, 1)
    out = []
    for ci in range(CPS):
        k = ci // 2
        if ci % 2 == 0:
            out.append(jnp.where(lane_lo, tile(lo, k), swap(tile(hi, k))))
        else:
            out.append(jnp.where(lane_lo, swap(tile(lo, k)), tile(hi, k)))
    return out


def _nseq(nb, most=2):
    n = most
    while nb % n:
        n //= 2
    return n


INPROJ_TN = MAIN_WIDTH // 2
ATTN_COL0 = AQ_BLK * ATTN_WIDTH - INPROJ_TN


def _inproj_kernel(x_ref, g_ref, w_ref, ws_ref, gq_ref, gk_ref, bd_ref,
                   main_ref, small_ref, qn_ref, kn_ref, vn_ref, h_ref):
    j = pl.program_id(1)

    @pl.when(j == 0)
    def _():
        x = x_ref[...]
        ms = jnp.mean(x * x, axis=-1, keepdims=True)
        h = (x * lax.rsqrt(ms + EPS) * g_ref[...]).astype(BF16)
        h_ref[...] = h
        small_ref[...] = _dot(h, ws_ref[...])

    acc = _dot(h_ref[...], w_ref[...])
    main_ref[...] = acc.astype(BF16)

    @pl.when(j == 1)
    def _():
        bd = bd_ref[...]
        q = acc[:, ATTN_COL0:ATTN_COL0 + ATTN_WIDTH]
        k = acc[:, ATTN_COL0 + ATTN_WIDTH:ATTN_COL0 + 2 * ATTN_WIDTH]
        qms = _dot((q * q).astype(BF16), bd)
        kms = _dot((k * k).astype(BF16), bd)
        qn_ref[...] = (q * lax.rsqrt(qms + EPS) * gq_ref[...] * (ATTN_HEAD_DIM ** -0.5)).astype(BF16)
        kn_ref[...] = (k * lax.rsqrt(kms + EPS) * gk_ref[...]).astype(BF16)
        vn_ref[...] = acc[:, ATTN_COL0 + 2 * ATTN_WIDTH:ATTN_COL0 + 3 * ATTN_WIDTH].astype(BF16)


def _inproj(x2, g, w_main, w_small, gq, gk, bd64):
    t = x2.shape[0]
    tm = min(1024, t)
    assert MAIN_WIDTH // INPROJ_TN == 2
    const = lambda shp: pl.BlockSpec(shp, lambda i, j: (0, 0))
    qkv_spec = pl.BlockSpec((tm, ATTN_WIDTH), lambda i, j: (i, 0))
    qkv_shape = jax.ShapeDtypeStruct((t, ATTN_WIDTH), BF16)
    return pl.pallas_call(
        _inproj_kernel,
        grid=(t // tm, 2),
        in_specs=[
            pl.BlockSpec((tm, D_MODEL), lambda i, j: (i, 0)),
            const((1, D_MODEL)),
            pl.BlockSpec((D_MODEL, INPROJ_TN), lambda i, j: (0, j)),
            const((D_MODEL, SMALL_WIDTH)),
            const((1, ATTN_WIDTH)), const((1, ATTN_WIDTH)), const((ATTN_WIDTH, ATTN_WIDTH)),
        ],
        out_specs=[
            pl.BlockSpec((tm, INPROJ_TN), lambda i, j: (i, j)),
            pl.BlockSpec((tm, SMALL_WIDTH), lambda i, j: (i, 0)),
            qkv_spec, qkv_spec, qkv_spec,
        ],
        out_shape=[
            jax.ShapeDtypeStruct((t, MAIN_WIDTH), BF16),
            jax.ShapeDtypeStruct((t, SMALL_WIDTH), F32),
            qkv_shape, qkv_shape, qkv_shape,
        ],
        scratch_shapes=[pltpu.VMEM((tm, D_MODEL), BF16)],
        compiler_params=pltpu.CompilerParams(
            dimension_semantics=("arbitrary", "arbitrary"), vmem_limit_bytes=VMEM_LIMIT),
        name="inproj",
    )(x2, g, w_main, w_small, gq, gk, bd64)


def _outproj_kernel(x_ref, ys_ref, ya_ref, yd_ref, ws_ref, wa_ref, wd_ref, o_ref):
    acc = _dot(ys_ref[...], ws_ref[...])
    acc = acc + _dot(ya_ref[...], wa_ref[...])
    acc = acc + _dot(yd_ref[...], wd_ref[...])
    o_ref[...] = x_ref[...] + acc


def _outproj(x2, y_ssd, y_attn, y_dn, wo_s, wo_a, wo_d):
    t = x2.shape[0]
    tm = min(1024, t)
    row = lambda w: pl.BlockSpec((tm, w), lambda i: (i, 0))
    full = lambda r: pl.BlockSpec((r, D_MODEL), lambda i: (0, 0))
    return pl.pallas_call(
        _outproj_kernel,
        grid=(t // tm,),
        in_specs=[row(D_MODEL), row(SSD_WIDTH), row(ATTN_WIDTH), row(DN_WIDTH),
                  full(SSD_WIDTH), full(ATTN_WIDTH), full(DN_WIDTH)],
        out_specs=row(D_MODEL),
        out_shape=jax.ShapeDtypeStruct((t, D_MODEL), F32),
        compiler_params=pltpu.CompilerParams(
            dimension_semantics=("arbitrary",), vmem_limit_bytes=VMEM_LIMIT),
        name="outproj",
    )(x2, y_ssd, y_attn, y_dn, wo_s, wo_a, wo_d)


def _ssd_prep(seq, first, xbc_ref, sm_ref, cw_ref, cb_ref, dtb_ref, alog_ref, e2_ref, ltri_ref,
              shift_ref, tail_ref):
    xa = _conv_silu(xbc_ref[seq], tail_ref, seq, shift_ref, cw_ref, cb_ref[...], first)
    xs = xa[:, 0:SSD_WIDTH]
    bm = xa[:, SSD_WIDTH:SSD_WIDTH + 256]
    cm = xa[:, SSD_WIDTH + 256:SSD_WIDTH + 512]

    dt = _softplus(sm_ref[seq] + dtb_ref[...])
    d_a = dt * (-jnp.exp(alog_ref[...]))
    acum = _dot2l(ltri_ref[...], d_a)
    acum_t = acum.T
    prow = _chunk_pairs(acum_t[0:8, :], acum_t[8:16, :])

    e2 = e2_ref[...]
    acum_x = _dot2k(acum, e2)
    dt_x = _dot2k(dt, e2)
    return dict(xs=xs, bm=bm, cm=cm, prow=prow, acum_x=acum_x, xdt=xs * dt_x, eacum=jnp.exp(acum_x))


def _ssd_chunk(pr, seq, ci, st_ref, dexp_ref, tri2, lane_lo):
    rs = slice(ci * CHUNK, (ci + 1) * CHUNK)
    ax = pr["acum_x"][rs]
    last = ax[CHUNK - 1:CHUNK, :]
    dec = jnp.exp(last - ax)
    cdec = jnp.exp(last)
    xdt_c = pr["xdt"][rs]
    xdd = (xdt_c * dec).astype(BF16)
    xdt_b = xdt_c.astype(BF16)
    pc = pr["prow"][ci]
    ydiag, yoff = [], []
    for g in range(SSD_GROUPS):
        gs = slice(g * 512, (g + 1) * 512)
        bg = pr["bm"][rs, g * SSD_STATE:(g + 1) * SSD_STATE]
        cg = pr["cm"][rs, g * SSD_STATE:(g + 1) * SSD_STATE].astype(BF16)
        bgt2 = jnp.concatenate([bg, bg], axis=0).T.astype(BF16)
        cb2 = _dot(cg, bgt2)
        st_g = st_ref[seq, :, gs]
        yoff.append(_dot(cg, st_g.astype(BF16)))
        new_g = _dot(bgt2[:, 0:CHUNK], xdd[:, gs])
        st_ref[seq, :, gs] = st_g * cdec[:, gs] + new_g
        for jp in range(4):
            j = g * 4 + jp
            tl = slice(j * LANES, (j + 1) * LANES)
            diff = ax[:, tl] - pc[j:j + 1, :]
            lp = jnp.exp(jnp.where(tri2, diff, NEG))
            mp = (cb2 * lp).astype(BF16)
            xt = xdt_b[:, tl]
            zx = jnp.zeros_like(xt)
            xbd = jnp.concatenate([jnp.where(lane_lo, xt, zx), jnp.where(lane_lo, zx, xt)],
                                  axis=0)
            ydiag.append(_dot(mp, xbd))
    return (jnp.concatenate(ydiag, axis=1) + jnp.concatenate(yoff, axis=1) * pr["eacum"][rs]
            + pr["xs"][rs] * dexp_ref[...])


def _ssd_kernel(xbc_ref, z_ref, sm_ref, cw_ref, cb_ref, dtb_ref, alog_ref, dexp_ref, ng_ref,
                e2_ref, ltri_ref, shift_ref, y_ref, tail_ref, st_ref, *, nseq):
    first = pl.program_id(1) == 0

    @pl.when(first)
    def _():
        st_ref[...] = jnp.zeros_like(st_ref)

    row_i = lax.broadcasted_iota(jnp.int32, (CHUNK, LANES), 0)
    lane = lax.broadcasted_iota(jnp.int32, (CHUNK, LANES), 1)
    lane_lo = lane < CHUNK
    tri2 = row_i >= jnp.where(lane_lo, lane, lane - CHUNK)

    prep = [_ssd_prep(seq, first, xbc_ref, sm_ref, cw_ref, cb_ref, dtb_ref, alog_ref, e2_ref,
                      ltri_ref, shift_ref, tail_ref) for seq in range(nseq)]
    ys = [[] for _ in range(nseq)]
    for ci in range(CPS):
        for seq in range(nseq):
            ys[seq].append(_ssd_chunk(prep[seq], seq, ci, st_ref, dexp_ref, tri2, lane_lo))
    for seq in range(nseq):
        y = jnp.concatenate(ys[seq], axis=0) * _silu(z_ref[seq].astype(F32))
        outs = []
        for g in range(SSD_GROUPS):
            gs = slice(g * 512, (g + 1) * 512)
            yg = y[:, gs]
            ms = jnp.mean(yg * yg, axis=-1, keepdims=True)
            outs.append(yg * lax.rsqrt(ms + EPS) * ng_ref[:, gs])
        y_ref[seq] = jnp.concatenate(outs, axis=1).astype(BF16)


def _ssd(main, small, nb, ns, cw, cb, dtb, alog, dexp, ng, consts):
    nseq = _nseq(nb, SSD_NSEQ)
    spb = ns // ROWS
    main3 = main.reshape(nb, ns, MAIN_WIDTH)
    small3 = small.reshape(nb, ns, SMALL_WIDTH)
    rowmap = lambda blk: (lambda b, c: (b, c, blk))
    const = lambda shp: pl.BlockSpec(shp, lambda b, c: (0,) * len(shp))
    y = pl.pallas_call(
        functools.partial(_ssd_kernel, nseq=nseq),
        grid=(nb // nseq, spb),
        in_specs=[
            pl.BlockSpec((nseq, ROWS, SSD_CONV_DIM), rowmap(XBC_BLK)),
            pl.BlockSpec((nseq, ROWS, SSD_WIDTH), rowmap(Z_BLK)),
            pl.BlockSpec((nseq, ROWS, SMALL_WIDTH), rowmap(0)),
            const((CONV_K, SSD_CONV_DIM)), const((1, SSD_CONV_DIM)),
            const((1, SMALL_WIDTH)), const((1, SMALL_WIDTH)),
            const((1, SSD_WIDTH)), const((1, SSD_WIDTH)),
            const((2 * SMALL_WIDTH, SSD_WIDTH)), const((ROWS, ROWS)),
            const(((CONV_K - 1) * ROWS, ROWS)),
        ],
        out_specs=pl.BlockSpec((nseq, ROWS, SSD_WIDTH), rowmap(0)),
        out_shape=jax.ShapeDtypeStruct((nb, ns, SSD_WIDTH), BF16),
        scratch_shapes=[pltpu.VMEM((nseq, TAIL, SSD_CONV_DIM), F32),
                        pltpu.VMEM((nseq, SSD_STATE, SSD_WIDTH), F32)],
        compiler_params=pltpu.CompilerParams(
            dimension_semantics=("arbitrary", "arbitrary"), vmem_limit_bytes=VMEM_LIMIT),
        name="ssd",
    )(main3, main3, small3, cw, cb, dtb, alog, dexp, ng,
      consts["e_ssd2"], consts["ltri"], consts["shift"])
    return y.reshape(nb * ns, SSD_WIDTH)


def _attn_kernel(q_ref, k_ref, v_ref, gate_ref, bias_ref, o_ref, *, nseq):
    qi = pl.program_id(1)
    start = pl.multiple_of(jnp.maximum(qi * ATTN_TQ - ATTN_PAD, 0), ATTN_TQ)
    lane_lo = lax.broadcasted_iota(jnp.int32, (ATTN_TQ, LANES), 1) < ATTN_HEAD_DIM
    lane_hi = jnp.logical_not(lane_lo)
    zq = jnp.zeros((ATTN_TQ, LANES), BF16)
    tile = lambda x, h: x[:, (h // 2) * LANES:(h // 2 + 1) * LANES]
    units = [(s, h) for s in range(nseq) for h in range(ATTN_HEADS)]
    q = [q_ref[s] for s in range(nseq)]
    kk = [k_ref[s, pl.ds(start, ATTN_NK), :] for s in range(nseq)]
    vv = [v_ref[s, pl.ds(start, ATTN_NK), :] for s in range(nseq)]
    qm = [jnp.where(lane_lo if h % 2 == 0 else lane_hi, tile(q[s], h), zq) for s, h in units]
    sc = [_dot_nt(qm[u], tile(kk[s], h)) + bias_ref[0, h] for u, (s, h) in enumerate(units)]
    mx = [jnp.max(x, axis=-1, keepdims=True) for x in sc]
    p = [jnp.exp(x - m) for x, m in zip(sc, mx)]
    den = [jnp.sum(x, axis=-1, keepdims=True) for x in p]
    ov = [_dot(p[u].astype(BF16), tile(vv[s], h)) / den[u] for u, (s, h) in enumerate(units)]
    for s in range(nseq):
        base = s * ATTN_HEADS
        outs = [jnp.where(lane_lo, ov[base + 2 * j], ov[base + 2 * j + 1]) for j in range(ATTN_HEADS // 2)]
        o = jnp.concatenate(outs, axis=1) * _silu(gate_ref[s].astype(F32))
        o_ref[s] = o.astype(BF16)


def _attn(qn, kn, vn, main, nb, ns, bias):
    nseq = _nseq(nb)
    spb = ns // ATTN_TQ
    assert ns >= ATTN_NK
    nvar = bias.shape[0]
    blk = lambda rows, cb: pl.BlockSpec((nseq, rows, ATTN_WIDTH), cb)
    y = pl.pallas_call(
        functools.partial(_attn_kernel, nseq=nseq),
        grid=(nb // nseq, spb),
        in_specs=[
            blk(ATTN_TQ, lambda b, i: (b, i, 0)),
            blk(ns, lambda b, i: (b, 0, 0)),
            blk(ns, lambda b, i: (b, 0, 0)),
            blk(ATTN_TQ, lambda b, i: (b, i, AGATE_BLK)),
            pl.BlockSpec((1, ATTN_HEADS, ATTN_TQ, ATTN_NK),
                         lambda b, i: (jnp.minimum(i, nvar - 1), 0, 0, 0)),
        ],
        out_specs=blk(ATTN_TQ, lambda b, i: (b, i, 0)),
        out_shape=jax.ShapeDtypeStruct((nb, ns, ATTN_WIDTH), BF16),
        compiler_params=pltpu.CompilerParams(
            dimension_semantics=("arbitrary", "arbitrary"), vmem_limit_bytes=VMEM_LIMIT),
        name="attn",
    )(qn.reshape(nb, ns, ATTN_WIDTH), kn.reshape(nb, ns, ATTN_WIDTH),
      vn.reshape(nb, ns, ATTN_WIDTH), main.reshape(nb, ns, MAIN_WIDTH), bias)
    return y.reshape(nb * ns, ATTN_WIDTH)


def _row_stack_masked(y, groups):
    z = jnp.zeros_like(y)
    return jnp.concatenate([jnp.where(g, y, z) for g in groups], axis=0)


def _tri_inv_minus_eye(a_list, eye, blk, grp):
    bf = lambda xs: [x.astype(BF16) for x in xs]
    bd = lambda xs: [_row_stack_masked(x, grp) for x in xs]
    mm = lambda xs, ys: [_dot(x, y) for x, y in zip(xs, ys)]
    add = lambda xs, ys: [x + y for x, y in zip(xs, ys)]
    d = [jnp.where(blk, a, 0.0) for a in a_list]
    o = [a - x for a, x in zip(a_list, d)]
    db = bf(d)
    d2b = bf(mm(db, bd(db)))
    d2bd = bd(d2b)
    d4b = bf(mm(d2b, d2bd))
    d4bd = bd(d4b)
    d8bd = bd(bf(mm(d4b, d4bd)))
    x = [eye - y for y in d]
    x = add(x, mm(bf(x), d2bd))
    x = add(x, mm(bf(x), d4bd))
    td = add(x, mm(bf(x), d8bd))
    tdm = [y - eye for y in td]
    tdmb = bf(tdm)
    n = add(o, mm(tdmb, bd(bf(o))))
    nb = bf(n)
    n2bd = bd(bf(mm(nb, bd(nb))))
    tnm = [m - y for m, y in zip(mm(bf([eye - y for y in n]), n2bd), n)]
    cross = mm(bf(tnm), bd(tdmb))
    return [p + q + r for p, q, r in zip(tdm, tnm, cross)]


def _gdn_prep(seq, first, qkv_ref, sm_ref, cw_ref, dtb_ref, alog_ref, eb2_ref, eg2_ref, eq2_ref,
              ltri_ref, shift_ref, tail_ref):
    qkv = _conv_silu(qkv_ref[seq], tail_ref, seq, shift_ref, cw_ref, None, first)
    q = qkv[:, 0:DN_WIDTH]
    k = qkv[:, DN_WIDTH:2 * DN_WIDTH]
    v = qkv[:, 2 * DN_WIDTH:3 * DN_WIDTH]
    q = _head_norm(q, DN_HEAD_DIM, 1.0) * (DN_HEAD_DIM ** -0.5)
    k = _head_norm(k, DN_HEAD_DIM, 1.0)

    sm = sm_ref[seq]
    beta_s = jax.nn.sigmoid(sm)
    g_s = -jnp.exp(alog_ref[...]) * _softplus(sm + dtb_ref[...])
    gcum = _dot2l(ltri_ref[...], g_s)
    g_t = gcum.T[16:24, :]
    grow = _chunk_pairs(g_t, pltpu.roll(g_t, 7, 0))
    gcol = _dot2k(gcum, eq2_ref[...])
    beta_x = _dot2k(beta_s, eb2_ref[...])
    gc_x = _dot2k(gcum, eg2_ref[...])
    egc = jnp.exp(gc_x)
    kb = k * beta_x
    return dict(q=q, k=k, kb=kb, vb=v * beta_x, kbg=kb * egc, qd=q * egc, gc_x=gc_x, gcol=gcol, grow=grow)


def _gdn_kernel(qkv_ref, gate_ref, sm_ref, cw_ref, dtb_ref, alog_ref, ng_ref, eb2_ref, eg2_ref,
                eq2_ref, ltri_ref, shift_ref, o_ref, tail_ref, st_ref,
                *, nseq):
    first = pl.program_id(1) == 0

    @pl.when(first)
    def _():
        st_ref[...] = jnp.zeros_like(st_ref)

    prep = [_gdn_prep(seq, first, qkv_ref, sm_ref, cw_ref, dtb_ref, alog_ref, eb2_ref, eg2_ref,
                      eq2_ref, ltri_ref, shift_ref, tail_ref)
            for seq in range(nseq)]

    qw = DN_HEADS * CHUNK
    rp = lax.broadcasted_iota(jnp.int32, (CHUNK, LANES), 0)
    lp = lax.broadcasted_iota(jnp.int32, (CHUNK, LANES), 1)
    cp = lp & (CHUNK - 1)
    tri = rp >= cp
    stri = rp > cp
    blk = (rp >> 4) == (cp >> 4)
    eye = jnp.where(rp == cp, 1.0, 0.0).astype(F32)
    grp_p = [lp < CHUNK, lp >= CHUNK]
    lpw = lax.broadcasted_iota(jnp.int32, (CHUNK, PAIR_W), 1)
    grp_pw = [lpw < DN_HEAD_DIM, lpw >= DN_HEAD_DIM]

    chunks = [slice(ci * CHUNK, (ci + 1) * CHUNK) for ci in range(CPS)]
    pairs = range(DN_HEADS // 2)
    units, a_list, attn_list = [], [], []
    for seq in range(nseq):
        pr = prep[seq]
        for ci, rs in enumerate(chunks):
            for p in pairs:
                ps = slice(p * PAIR_W, (p + 1) * PAIR_W)
                kstack = _row_stack_masked(pr["k"][rs, ps].astype(BF16), grp_pw)
                lhs = jnp.concatenate([pr["kb"][rs, ps], pr["q"][rs, ps]], axis=0).astype(BF16)
                kq = _dot_nt(lhs, kstack)
                g_row = pr["grow"][ci][DNA_COL - 16 + 2 * p:DNA_COL - 15 + 2 * p, :]
                diff = pr["gcol"][rs, p * LANES:(p + 1) * LANES] - g_row
                dm = jnp.exp(jnp.where(tri, diff, NEG))
                units.append((seq, ci, p))
                a_list.append(jnp.where(stri, kq[0:CHUNK] * dm, 0.0))
                attn_list.append((kq[CHUNK:2 * CHUNK] * dm).astype(BF16))
    tm_list = _tri_inv_minus_eye(a_list, eye, blk, grp_p)
    u_list, w_list = [], []
    for (seq, ci, p), tm in zip(units, tm_list):
        rs = chunks[ci]
        ps = slice(p * PAIR_W, (p + 1) * PAIR_W)
        tmb = tm.astype(BF16)
        vb_p = prep[seq]["vb"][rs, ps]
        kbg_p = prep[seq]["kbg"][rs, ps]
        u_list.append(vb_p + _dot(tmb, _row_stack_masked(vb_p.astype(BF16), grp_pw)))
        w_list.append(kbg_p + _dot(tmb, _row_stack_masked(kbg_p.astype(BF16), grp_pw)))
    unit_of = {u: i for i, u in enumerate(units)}

    outs = [[] for _ in range(nseq)]
    for ci, rs in enumerate(chunks):
        for seq in range(nseq):
            pr = prep[seq]
            gx = pr["gc_x"][rs]
            glast = gx[CHUNK - 1:CHUNK, :]
            kd = pr["k"][rs] * jnp.exp(glast - gx)
            sdec = jnp.exp(glast)
            out_p = []
            for p in pairs:
                i = unit_of[(seq, ci, p)]
                wq = []
                for hh in range(2):
                    h = 2 * p + hh
                    th = slice(h * DN_HEAD_DIM, (h + 1) * DN_HEAD_DIM)
                    tp = slice(hh * DN_HEAD_DIM, (hh + 1) * DN_HEAD_DIM)
                    lhs = jnp.concatenate([w_list[i][:, tp], pr["qd"][rs, th]], axis=0).astype(BF16)
                    wq.append(_dot(lhs, st_ref[seq, h].astype(BF16)))
                vn = u_list[i] - jnp.concatenate([x[0:CHUNK] for x in wq], axis=1)
                vnb = vn.astype(BF16)
                ha, hb = 2 * p, 2 * p + 1
                kd_t = jnp.concatenate([kd[:, ha * DN_HEAD_DIM:(ha + 1) * DN_HEAD_DIM],
                                        kd[:, hb * DN_HEAD_DIM:(hb + 1) * DN_HEAD_DIM]], axis=0).T
                lhs = jnp.concatenate([attn_list[i], kd_t.astype(BF16)], axis=0)
                res = _dot(lhs, _row_stack_masked(vnb, grp_pw))
                out_p.append(jnp.concatenate([x[CHUNK:2 * CHUNK] for x in wq], axis=1) + res[0:CHUNK])
                for hh in range(2):
                    h = 2 * p + hh
                    th = slice(h * DN_HEAD_DIM, (h + 1) * DN_HEAD_DIM)
                    tp = slice(hh * DN_HEAD_DIM, (hh + 1) * DN_HEAD_DIM)
                    st_ref[seq, h] = st_ref[seq, h] * sdec[:, th] + res[CHUNK:CHUNK + DN_HEAD_DIM, tp]
            outs[seq].append(jnp.concatenate(out_p, axis=1))
    for seq in range(nseq):
        o = jnp.concatenate(outs[seq], axis=0)
        o = _head_norm(o, DN_HEAD_DIM, 1.0 / DN_HEAD_DIM) * ng_ref[...]
        o_ref[seq] = (o * _silu(gate_ref[seq].astype(F32))).astype(BF16)


def _gdn(main, small, nb, ns, cw, dtb, alog, ng, consts):
    nseq = _nseq(nb, GDN_NSEQ)
    spb = ns // ROWS
    main3 = main.reshape(nb, ns, MAIN_WIDTH)
    small3 = small.reshape(nb, ns, SMALL_WIDTH)
    rowmap = lambda blk: (lambda b, c: (b, c, blk))
    const = lambda shp: pl.BlockSpec(shp, lambda b, c: (0,) * len(shp))
    qw = DN_HEADS * CHUNK
    y = pl.pallas_call(
        functools.partial(_gdn_kernel, nseq=nseq),
        grid=(nb // nseq, spb),
        in_specs=[
            pl.BlockSpec((nseq, ROWS, DN_CONV_DIM), rowmap(DNQKV_BLK)),
            pl.BlockSpec((nseq, ROWS, DN_WIDTH), rowmap(DNGATE_BLK)),
            pl.BlockSpec((nseq, ROWS, SMALL_WIDTH), rowmap(0)),
            const((CONV_K, DN_CONV_DIM)),
            const((1, SMALL_WIDTH)), const((1, SMALL_WIDTH)), const((1, DN_WIDTH)),
            const((2 * SMALL_WIDTH, DN_WIDTH)), const((2 * SMALL_WIDTH, DN_WIDTH)),
            const((2 * SMALL_WIDTH, qw)),
            const((ROWS, ROWS)),
            const(((CONV_K - 1) * ROWS, ROWS)),
        ],
        out_specs=pl.BlockSpec((nseq, ROWS, DN_WIDTH), rowmap(0)),
        out_shape=jax.ShapeDtypeStruct((nb, ns, DN_WIDTH), BF16),
        scratch_shapes=[pltpu.VMEM((nseq, TAIL, DN_CONV_DIM), F32),
                        pltpu.VMEM((nseq, DN_HEADS, DN_HEAD_DIM, DN_HEAD_DIM), F32)],
        compiler_params=pltpu.CompilerParams(
            dimension_semantics=("arbitrary", "arbitrary"), vmem_limit_bytes=VMEM_LIMIT),
        name="gdn",
    )(main3, main3, small3, cw, dtb, alog, ng,
      consts["e_beta2"], consts["e_g2"], consts["e_gq2"], consts["ltri"],
      consts["shift"])
    return y.reshape(nb * ns, DN_WIDTH)


_HEAD_PERM = np.concatenate([np.arange(0, SSD_HEADS, 2), np.arange(1, SSD_HEADS, 2)])


def _constants():
    r = np.arange(ROWS)
    same_chunk = (r[:, None] // CHUNK) == (r[None, :] // CHUNK)
    ltri = (same_chunk & (r[None, :] <= r[:, None])).astype(np.float32)
    e_ssd = np.zeros((SMALL_WIDTH, SSD_WIDTH), np.float32)
    for col, h in enumerate(_HEAD_PERM):
        e_ssd[col, h * SSD_HEAD_DIM:(h + 1) * SSD_HEAD_DIM] = 1.0
    qw = DN_HEADS * CHUNK
    e_beta = np.zeros((SMALL_WIDTH, DN_WIDTH), np.float32)
    e_g = np.zeros((SMALL_WIDTH, DN_WIDTH), np.float32)
    e_gq = np.zeros((SMALL_WIDTH, qw), np.float32)
    for h in range(DN_HEADS):
        e_beta[DNB_COL + h, h * DN_HEAD_DIM:(h + 1) * DN_HEAD_DIM] = 1.0
        e_g[DNA_COL + h, h * DN_HEAD_DIM:(h + 1) * DN_HEAD_DIM] = 1.0
        e_gq[DNA_COL + h, h * CHUNK:(h + 1) * CHUNK] = 1.0
    lane = np.arange(ATTN_WIDTH)
    bd64 = ((lane[:, None] // ATTN_HEAD_DIM) == (lane[None, :] // ATTN_HEAD_DIM)).astype(np.float32)
    bd64 = bd64 / ATTN_HEAD_DIM
    shift = np.zeros(((CONV_K - 1) * ROWS, ROWS), np.float32)
    for tap in range(CONV_K - 1):
        s = CONV_K - 1 - tap
        shift[tap * ROWS + np.arange(s, ROWS), np.arange(0, ROWS - s)] = 1.0
    twice = lambda m: np.concatenate([m, m], axis=0)
    out = dict(ltri=ltri, e_ssd2=twice(e_ssd), e_beta2=twice(e_beta), e_g2=twice(e_g),
               e_gq2=twice(e_gq), bd64=bd64, shift=shift)
    return {k: jnp.asarray(v, BF16) for k, v in out.items()}


def _attn_bias(rel_bias):
    r = np.arange(ATTN_TQ)[:, None]
    c = np.arange(ATTN_NK)[None, :]
    qc, kc = r // CHUNK, c // CHUNK
    valid = (kc >= qc) & (kc <= qc + ATTN_BAND_CHUNKS - 1)
    nh = rel_bias.shape[0]
    assert ATTN_TQ - 1 <= ATTN_REL_CLIP
    near = rel_bias[:, ATTN_REL_CLIP - (ATTN_TQ - 1):2 * ATTN_REL_CLIP]
    far = jnp.broadcast_to(rel_bias[:, 2 * ATTN_REL_CLIP:], (nh, ATTN_NK - ATTN_REL_CLIP))
    t = jnp.concatenate([near, far], axis=1)
    span = ATTN_TQ + ATTN_NK - 1
    w = jnp.concatenate([t[:, ::-1], jnp.zeros((nh, 1), rel_bias.dtype)], axis=1)
    skew = jnp.tile(w, (1, ATTN_TQ))[:, :ATTN_TQ * span].reshape(nh, ATTN_TQ, span)
    bias = skew[:, :, ATTN_TQ - 1:ATTN_TQ - 1 + ATTN_NK]
    bias = jnp.where(valid[None], bias, NEG).astype(F32)
    ext = jnp.concatenate([bias, jnp.full((nh, ATTN_TQ, ATTN_PAD), NEG, F32)], axis=2)
    offs = [max(ATTN_PAD - v * ATTN_TQ, 0) for v in range(ATTN_PAD // ATTN_TQ + 1)]
    return jnp.stack([ext[:, :, o:o + ATTN_NK] for o in offs], axis=0)


def _pad_row(v, start):
    return jnp.zeros((1, SMALL_WIDTH), F32).at[0, start:start + v.shape[0]].set(v.astype(F32))


def kernel(x, norm_g, w_in, ssd_conv_w, ssd_conv_b, ssd_dt_bias, ssd_a_log, ssd_d, ssd_norm_g,
           attn_q_norm_g, attn_k_norm_g, attn_rel_bias, dn_conv_w, dn_dt_bias, dn_a_log,
           dn_norm_g, w_out):
    nb, ns, _ = x.shape
    depth = w_in.shape[0]
    assert ns % ROWS == 0 and ns % ATTN_PAD == 0
    consts = _constants()
    x2 = x.reshape(nb * ns, D_MODEL)
    for l in range(depth):
        w = w_in[l]
        w_main = jnp.concatenate(
            [w[:, 1024:2560], w[:, 4624:6160], w[:, 0:1024], w[:, 2576:4624], w[:, 6160:6672]],
            axis=1).astype(BF16)
        w_small = jnp.concatenate(
            [w[:, 2560:2576][:, _HEAD_PERM], w[:, 6672:6680],
             jnp.zeros((D_MODEL, SMALL_WIDTH - 24), F32)], axis=1).astype(BF16)
        main, small, qn, kn, vn = _inproj(
            x2, norm_g[l][None, :], w_main, w_small,
            jnp.tile(attn_q_norm_g[l], ATTN_HEADS)[None, :],
            jnp.tile(attn_k_norm_g[l], ATTN_HEADS)[None, :], consts["bd64"])

        y_ssd = _ssd(main, small, nb, ns, ssd_conv_w[l], ssd_conv_b[l][None, :],
                     _pad_row(ssd_dt_bias[l][_HEAD_PERM], 0), _pad_row(ssd_a_log[l][_HEAD_PERM], 0),
                     jnp.repeat(ssd_d[l].astype(F32), SSD_HEAD_DIM)[None, :],
                     ssd_norm_g[l][None, :], consts)

        y_attn = _attn(qn, kn, vn, main, nb, ns, _attn_bias(attn_rel_bias[l]))

        y_dn = _gdn(main, small, nb, ns, dn_conv_w[l], _pad_row(dn_dt_bias[l], DNA_COL),
                    _pad_row(dn_a_log[l], DNA_COL), jnp.tile(dn_norm_g[l], DN_HEADS)[None, :], consts)

        wo = w_out[l].astype(BF16)
        x2 = _outproj(x2, y_ssd, y_attn, y_dn, wo[0:1024], wo[1024:1536], wo[1536:2048])
    return x2.reshape(nb, ns, D_MODEL)
```

```python
import functools

import numpy as np
import jax
import jax.numpy as jnp
from jax import lax
from jax.experimental import pallas as pl
from jax.experimental.pallas import tpu as pltpu

F32 = jnp.float32
BF16 = jnp.bfloat16

D_MODEL = 1024
CHUNK = 64
EPS = 1e-6
CONV_K = 4
SSD_WIDTH = 1024
SSD_HEADS = 16
SSD_HEAD_DIM = 64
SSD_GROUPS = 2
SSD_STATE = 128
SSD_CONV_DIM = 1536
ATTN_WIDTH = 512
ATTN_HEADS = 8
ATTN_HEAD_DIM = 64
ATTN_BAND_CHUNKS = 9
ATTN_REL_CLIP = 128
DN_WIDTH = 512
DN_HEADS = 4
DN_HEAD_DIM = 128
DN_CONV_DIM = 1536
MIX_WIDTH = 2048

MAIN_WIDTH = 6656
XBC_BLK = 0
DNQKV_BLK = 1
Z_BLK = 3
AQ_BLK, AK_BLK, AV_BLK, AGATE_BLK = 8, 9, 10, 11
DNGATE_BLK = 12
SMALL_WIDTH = 128
DNB_COL = 16
DNA_COL = 20

LANES = 128
NEG = -1e30
VMEM_LIMIT = 56 * 1024 * 1024

ROWS = 256
CPS = ROWS // CHUNK
TAIL = 8
ATTN_TQ = 128
ATTN_NK = ATTN_TQ + (ATTN_BAND_CHUNKS - 1) * CHUNK
ATTN_PAD = (ATTN_BAND_CHUNKS - 1) * CHUNK
PAIR_W = 2 * DN_HEAD_DIM
GDN_NSEQ = 4
SSD_NSEQ = 4


def _dot(a, b):
    return jnp.dot(a, b, preferred_element_type=F32)


def _dot_nt(a, b):
    return lax.dot_general(a, b, (((1,), (1,)), ((), ())), preferred_element_type=F32)


def _split(v):
    hi = v.astype(BF16)
    lo = (v - hi.astype(F32)).astype(BF16)
    return hi, lo


def _dot2(v, m):
    hi, lo = _split(v)
    return _dot(hi, m) + _dot(lo, m)


def _dot2k(v, m2):
    hi, lo = _split(v)
    return _dot(jnp.concatenate([hi, lo], axis=1), m2)


def _dot2l(m, v):
    hi, lo = _split(v)
    return _dot(m, hi) + _dot(m, lo)


def _silu(x):
    return x * jax.nn.sigmoid(x)


def _softplus(x):
    return jnp.maximum(x, 0.0) + jnp.log1p(jnp.exp(-jnp.abs(x)))


def _conv_silu(xb, tail_ref, seq, shift_ref, cw_ref, bias, first):
    rows = xb.shape[0]

    @pl.when(first)
    def _():
        tail_ref[seq] = jnp.zeros(tail_ref.shape[1:], tail_ref.dtype)

    tail = tail_ref[seq]
    xf = xb.astype(F32)
    acc = cw_ref[CONV_K - 1:CONV_K, :] * xf
    row = lax.broadcasted_iota(jnp.int32, tail.shape, 0)
    shifted = _dot(shift_ref[...], xb)
    head = None
    for t in range(CONV_K - 1):
        s = CONV_K - 1 - t
        w_t = cw_ref[t:t + 1, :]
        acc = acc + w_t * shifted[t * rows:(t + 1) * rows]
        corr = w_t * jnp.where(row < s, pltpu.roll(tail, s, 0), 0.0)
        head = corr if head is None else head + corr
    acc = jnp.concatenate([acc[0:TAIL] + head, acc[TAIL:]], axis=0)
    if bias is not None:
        acc = acc + bias
    tail_ref[seq] = xf[rows - TAIL:rows, :]
    return _silu(acc)


def _head_norm(x, width, scale):
    parts = []
    for h in range(x.shape[1] // width):
        xh = x[:, h * width:(h + 1) * width]
        ss = jnp.sum(xh * xh, axis=-1, keepdims=True)
        parts.append(xh * lax.rsqrt(ss * scale + EPS))
    return jnp.concatenate(parts, axis=1)


def _chunk_pairs(lo, hi):
    lane_lo = lax.broadcasted_iota(jnp.int32, (8, LANES), 1) < CHUNK
    tile = lambda x, k: x[:, k * LANES:(k + 1) * LANES]
    swap = lambda x: pltpu.roll(x, CHUNK, 1)
    out = []
    for ci in range(CPS):
        k = ci // 2
        if ci % 2 == 0:
            out.append(jnp.where(lane_lo, tile(lo, k), swap(tile(hi, k))))
        else:
            out.append(jnp.where(lane_lo, swap(tile(lo, k)), tile(hi, k)))
    return out


def _nseq(nb, most=2):
    n = most
    while nb % n:
        n //= 2
    return n


INPROJ_TN = MAIN_WIDTH // 2
ATTN_COL0 = AQ_BLK * ATTN_WIDTH - INPROJ_TN


def _inproj_kernel(x_ref, g_ref, w_ref, ws_ref, gq_ref, gk_ref, bd_ref,
                   main_ref, small_ref, qn_ref, kn_ref, vn_ref, h_ref):
    j = pl.program_id(1)

    @pl.when(j == 0)
    def _():
        x = x_ref[...]
        ms = jnp.mean(x * x, axis=-1, keepdims=True)
        h = (x * lax.rsqrt(ms + EPS) * g_ref[...]).astype(BF16)
        h_ref[...] = h
        small_ref[...] = _dot(h, ws_ref[...])

    acc = _dot(h_ref[...], w_ref[...])
    main_ref[...] = acc.astype(BF16)

    @pl.when(j == 1)
    def _():
        bd = bd_ref[...]
        q = acc[:, ATTN_COL0:ATTN_COL0 + ATTN_WIDTH]
        k = acc[:, ATTN_COL0 + ATTN_WIDTH:ATTN_COL0 + 2 * ATTN_WIDTH]
        qms = _dot((q * q).astype(BF16), bd)
        kms = _dot((k * k).astype(BF16), bd)
        qn_ref[...] = (q * lax.rsqrt(qms + EPS) * gq_ref[...] * (ATTN_HEAD_DIM ** -0.5)).astype(BF16)
        kn_ref[...] = (k * lax.rsqrt(kms + EPS) * gk_ref[...]).astype(BF16)
        vn_ref[...] = acc[:, ATTN_COL0 + 2 * ATTN_WIDTH:ATTN_COL0 + 3 * ATTN_WIDTH].astype(BF16)


def _inproj(x2, g, w_main, w_small, layer, gq, gk, bd64):
    t = x2.shape[0]
    tm = min(1024, t)
    assert MAIN_WIDTH // INPROJ_TN == 2
    const = lambda shp: pl.BlockSpec(shp, lambda i, j: (0, 0))
    qkv_spec = pl.BlockSpec((tm, ATTN_WIDTH), lambda i, j: (i, 0))
    qkv_shape = jax.ShapeDtypeStruct((t, ATTN_WIDTH), BF16)
    return pl.pallas_call(
        _inproj_kernel,
        grid=(t // tm, 2),
        in_specs=[
            pl.BlockSpec((tm, D_MODEL), lambda i, j: (i, 0)),
            const((1, D_MODEL)),
            pl.BlockSpec((None, D_MODEL, INPROJ_TN), lambda i, j: (layer, 0, j)),
            pl.BlockSpec((None, D_MODEL, SMALL_WIDTH), lambda i, j: (layer, 0, 0)),
            const((1, ATTN_WIDTH)), const((1, ATTN_WIDTH)), const((ATTN_WIDTH, ATTN_WIDTH)),
        ],
        out_specs=[
            pl.BlockSpec((tm, INPROJ_TN), lambda i, j: (i, j)),
            pl.BlockSpec((tm, SMALL_WIDTH), lambda i, j: (i, 0)),
            qkv_spec, qkv_spec, qkv_spec,
        ],
        out_shape=[
            jax.ShapeDtypeStruct((t, MAIN_WIDTH), BF16),
            jax.ShapeDtypeStruct((t, SMALL_WIDTH), F32),
            qkv_shape, qkv_shape, qkv_shape,
        ],
        scratch_shapes=[pltpu.VMEM((tm, D_MODEL), BF16)],
        compiler_params=pltpu.CompilerParams(
            dimension_semantics=("arbitrary", "arbitrary"), vmem_limit_bytes=VMEM_LIMIT),
        name="inproj",
    )(x2, g, w_main, w_small, gq, gk, bd64)


def _outproj_kernel(x_ref, ys_ref, ya_ref, yd_ref, ws_ref, wa_ref, wd_ref, o_ref):
    acc = _dot(ys_ref[...], ws_ref[...])
    acc = acc + _dot(ya_ref[...], wa_ref[...])
    acc = acc + _dot(yd_ref[...], wd_ref[...])
    o_ref[...] = x_ref[...] + acc


def _outproj(x2, y_ssd, y_attn, y_dn, wo, layer):
    t = x2.shape[0]
    tm = min(1024, t)
    row = lambda w: pl.BlockSpec((tm, w), lambda i: (i, 0))
    slab = lambda r, blk: pl.BlockSpec((None, r, D_MODEL), lambda i: (layer, blk, 0))
    assert SSD_WIDTH % ATTN_WIDTH == 0 and ATTN_WIDTH == DN_WIDTH
    return pl.pallas_call(
        _outproj_kernel,
        grid=(t // tm,),
        in_specs=[row(D_MODEL), row(SSD_WIDTH), row(ATTN_WIDTH), row(DN_WIDTH),
                  slab(SSD_WIDTH, 0), slab(ATTN_WIDTH, SSD_WIDTH // ATTN_WIDTH),
                  slab(DN_WIDTH, SSD_WIDTH // ATTN_WIDTH + 1)],
        out_specs=row(D_MODEL),
        out_shape=jax.ShapeDtypeStruct((t, D_MODEL), F32),
        compiler_params=pltpu.CompilerParams(
            dimension_semantics=("arbitrary",), vmem_limit_bytes=VMEM_LIMIT),
        name="outproj",
    )(x2, y_ssd, y_attn, y_dn, wo, wo, wo)


def _ssd_prep(seq, first, xbc_ref, sm_ref, cw_ref, cb_ref, dtb_ref, alog_ref, e2_ref, ltri_ref,
              shift_ref, tail_ref):
    xa = _conv_silu(xbc_ref[seq], tail_ref, seq, shift_ref, cw_ref, cb_ref[...], first)
    xs = xa[:, 0:SSD_WIDTH]
    bm = xa[:, SSD_WIDTH:SSD_WIDTH + 256]
    cm = xa[:, SSD_WIDTH + 256:SSD_WIDTH + 512]

    dt = _softplus(sm_ref[seq] + dtb_ref[...])
    d_a = dt * (-jnp.exp(alog_ref[...]))
    acum = _dot2l(ltri_ref[...], d_a)
    acum_t = acum.T
    prow = _chunk_pairs(acum_t[0:8, :], acum_t[8:16, :])

    e2 = e2_ref[...]
    acum_x = _dot2k(acum, e2)
    dt_x = _dot2k(dt, e2)
    return dict(xs=xs, bm=bm, cm=cm, prow=prow, acum_x=acum_x, xdt=xs * dt_x, eacum=jnp.exp(acum_x))


def _ssd_chunk(pr, seq, ci, st_ref, dexp_ref, tri2, lane_lo):
    rs = slice(ci * CHUNK, (ci + 1) * CHUNK)
    ax = pr["acum_x"][rs]
    last = ax[CHUNK - 1:CHUNK, :]
    dec = jnp.exp(last - ax)
    cdec = jnp.exp(last)
    xdt_c = pr["xdt"][rs]
    xdd = (xdt_c * dec).astype(BF16)
    xdt_b = xdt_c.astype(BF16)
    pc = pr["prow"][ci]
    ydiag, yoff = [], []
    for g in range(SSD_GROUPS):
        gs = slice(g * 512, (g + 1) * 512)
        bg = pr["bm"][rs, g * SSD_STATE:(g + 1) * SSD_STATE]
        cg = pr["cm"][rs, g * SSD_STATE:(g + 1) * SSD_STATE].astype(BF16)
        bgt2 = jnp.concatenate([bg, bg], axis=0).T.astype(BF16)
        cb2 = _dot(cg, bgt2)
        st_g = st_ref[seq, :, gs]
        yoff.append(_dot(cg, st_g.astype(BF16)))
        new_g = _dot(bgt2[:, 0:CHUNK], xdd[:, gs])
        st_ref[seq, :, gs] = st_g * cdec[:, gs] + new_g
        for jp in range(4):
            j = g * 4 + jp
            tl = slice(j * LANES, (j + 1) * LANES)
            diff = ax[:, tl] - pc[j:j + 1, :]
            lp = jnp.exp(jnp.where(tri2, diff, NEG))
            mp = (cb2 * lp).astype(BF16)
            xt = xdt_b[:, tl]
            zx = jnp.zeros_like(xt)
            xbd = jnp.concatenate([jnp.where(lane_lo, xt, zx), jnp.where(lane_lo, zx, xt)],
                                  axis=0)
            ydiag.append(_dot(mp, xbd))
    return (jnp.concatenate(ydiag, axis=1) + jnp.concatenate(yoff, axis=1) * pr["eacum"][rs]
            + pr["xs"][rs] * dexp_ref[...])


def _ssd_kernel(xbc_ref, z_ref, sm_ref, cw_ref, cb_ref, dtb_ref, alog_ref, dexp_ref, ng_ref,
                e2_ref, ltri_ref, shift_ref, y_ref, tail_ref, st_ref, *, nseq):
    first = pl.program_id(1) == 0

    @pl.when(first)
    def _():
        st_ref[...] = jnp.zeros_like(st_ref)

    row_i = lax.broadcasted_iota(jnp.int32, (CHUNK, LANES), 0)
    lane = lax.broadcasted_iota(jnp.int32, (CHUNK, LANES), 1)
    lane_lo = lane < CHUNK
    tri2 = row_i >= jnp.where(lane_lo, lane, lane - CHUNK)

    prep = [_ssd_prep(seq, first, xbc_ref, sm_ref, cw_ref, cb_ref, dtb_ref, alog_ref, e2_ref,
                      ltri_ref, shift_ref, tail_ref) for seq in range(nseq)]
    ys = [[] for _ in range(nseq)]
    for ci in range(CPS):
        for seq in range(nseq):
            ys[seq].append(_ssd_chunk(prep[seq], seq, ci, st_ref, dexp_ref, tri2, lane_lo))
    for seq in range(nseq):
        y = jnp.concatenate(ys[seq], axis=0) * _silu(z_ref[seq].astype(F32))
        outs = []
        for g in range(SSD_GROUPS):
            gs = slice(g * 512, (g + 1) * 512)
            yg = y[:, gs]
            ms = jnp.mean(yg * yg, axis=-1, keepdims=True)
            outs.append(yg * lax.rsqrt(ms + EPS) * ng_ref[:, gs])
        y_ref[seq] = jnp.concatenate(outs, axis=1).astype(BF16)


def _ssd(main, small, nb, ns, cw, cb, dtb, alog, dexp, ng, consts):
    nseq = _nseq(nb, SSD_NSEQ)
    spb = ns // ROWS
    main3 = main.reshape(nb, ns, MAIN_WIDTH)
    small3 = small.reshape(nb, ns, SMALL_WIDTH)
    rowmap = lambda blk: (lambda b, c: (b, c, blk))
    const = lambda shp: pl.BlockSpec(shp, lambda b, c: (0,) * len(shp))
    y = pl.pallas_call(
        functools.partial(_ssd_kernel, nseq=nseq),
        grid=(nb // nseq, spb),
        in_specs=[
            pl.BlockSpec((nseq, ROWS, SSD_CONV_DIM), rowmap(XBC_BLK)),
            pl.BlockSpec((nseq, ROWS, SSD_WIDTH), rowmap(Z_BLK)),
            pl.BlockSpec((nseq, ROWS, SMALL_WIDTH), rowmap(0)),
            const((CONV_K, SSD_CONV_DIM)), const((1, SSD_CONV_DIM)),
            const((1, SMALL_WIDTH)), const((1, SMALL_WIDTH)),
            const((1, SSD_WIDTH)), const((1, SSD_WIDTH)),
            const((2 * SMALL_WIDTH, SSD_WIDTH)), const((ROWS, ROWS)),
            const(((CONV_K - 1) * ROWS, ROWS)),
        ],
        out_specs=pl.BlockSpec((nseq, ROWS, SSD_WIDTH), rowmap(0)),
        out_shape=jax.ShapeDtypeStruct((nb, ns, SSD_WIDTH), BF16),
        scratch_shapes=[pltpu.VMEM((nseq, TAIL, SSD_CONV_DIM), F32),
                        pltpu.VMEM((nseq, SSD_STATE, SSD_WIDTH), F32)],
        compiler_params=pltpu.CompilerParams(
            dimension_semantics=("arbitrary", "arbitrary"), vmem_limit_bytes=VMEM_LIMIT),
        name="ssd",
    )(main3, main3, small3, cw, cb, dtb, alog, dexp, ng,
      consts["e_ssd2"], consts["ltri"], consts["shift"])
    return y.reshape(nb * ns, SSD_WIDTH)


def _attn_kernel(q_ref, k_ref, v_ref, gate_ref, bias_ref, o_ref, *, nseq):
    qi = pl.program_id(1)
    start = pl.multiple_of(jnp.maximum(qi * ATTN_TQ - ATTN_PAD, 0), ATTN_TQ)
    lane_lo = lax.broadcasted_iota(jnp.int32, (ATTN_TQ, LANES), 1) < ATTN_HEAD_DIM
    lane_hi = jnp.logical_not(lane_lo)
    zq = jnp.zeros((ATTN_TQ, LANES), BF16)
    tile = lambda x, h: x[:, (h // 2) * LANES:(h // 2 + 1) * LANES]
    units = [(s, h) for s in range(nseq) for h in range(ATTN_HEADS)]
    q = [q_ref[s] for s in range(nseq)]
    kk = [k_ref[s, pl.ds(start, ATTN_NK), :] for s in range(nseq)]
    vv = [v_ref[s, pl.ds(start, ATTN_NK), :] for s in range(nseq)]
    qm = [jnp.where(lane_lo if h % 2 == 0 else lane_hi, tile(q[s], h), zq) for s, h in units]
    sc = [_dot_nt(qm[u], tile(kk[s], h)) + bias_ref[0, h] for u, (s, h) in enumerate(units)]
    mx = [jnp.max(x, axis=-1, keepdims=True) for x in sc]
    p = [jnp.exp(x - m) for x, m in zip(sc, mx)]
    den = [jnp.sum(x, axis=-1, keepdims=True) for x in p]
    ov = [_dot(p[u].astype(BF16), tile(vv[s], h)) / den[u] for u, (s, h) in enumerate(units)]
    for s in range(nseq):
        base = s * ATTN_HEADS
        outs = [jnp.where(lane_lo, ov[base + 2 * j], ov[base + 2 * j + 1]) for j in range(ATTN_HEADS // 2)]
        o = jnp.concatenate(outs, axis=1) * _silu(gate_ref[s].astype(F32))
        o_ref[s] = o.astype(BF16)


def _attn(qn, kn, vn, main, nb, ns, bias, layer):
    nseq = _nseq(nb)
    spb = ns // ATTN_TQ
    assert ns >= ATTN_NK
    nvar = bias.shape[0]
    blk = lambda rows, cb: pl.BlockSpec((nseq, rows, ATTN_WIDTH), cb)
    y = pl.pallas_call(
        functools.partial(_attn_kernel, nseq=nseq),
        grid=(nb // nseq, spb),
        in_specs=[
            blk(ATTN_TQ, lambda b, i: (b, i, 0)),
            blk(ns, lambda b, i: (b, 0, 0)),
            blk(ns, lambda b, i: (b, 0, 0)),
            blk(ATTN_TQ, lambda b, i: (b, i, AGATE_BLK)),
            pl.BlockSpec((1, ATTN_HEADS, ATTN_TQ, ATTN_NK),
                         lambda b, i: (jnp.minimum(i, nvar - 1), layer, 0, 0)),
        ],
        out_specs=blk(ATTN_TQ, lambda b, i: (b, i, 0)),
        out_shape=jax.ShapeDtypeStruct((nb, ns, ATTN_WIDTH), BF16),
        compiler_params=pltpu.CompilerParams(
            dimension_semantics=("arbitrary", "arbitrary"), vmem_limit_bytes=VMEM_LIMIT),
        name="attn",
    )(qn.reshape(nb, ns, ATTN_WIDTH), kn.reshape(nb, ns, ATTN_WIDTH),
      vn.reshape(nb, ns, ATTN_WIDTH), main.reshape(nb, ns, MAIN_WIDTH), bias)
    return y.reshape(nb * ns, ATTN_WIDTH)


def _row_stack_masked(y, groups):
    z = jnp.zeros_like(y)
    return jnp.concatenate([jnp.where(g, y, z) for g in groups], axis=0)


def _tri_inv_minus_eye(a_list, eye, blk, grp):
    bf = lambda xs: [x.astype(BF16) for x in xs]
    bd = lambda xs: [_row_stack_masked(x, grp) for x in xs]
    mm = lambda xs, ys: [_dot(x, y) for x, y in zip(xs, ys)]
    add = lambda xs, ys: [x + y for x, y in zip(xs, ys)]
    d = [jnp.where(blk, a, 0.0) for a in a_list]
    o = [a - x for a, x in zip(a_list, d)]
    db = bf(d)
    d2b = bf(mm(db, bd(db)))
    d2bd = bd(d2b)
    d4b = bf(mm(d2b, d2bd))
    d4bd = bd(d4b)
    d8bd = bd(bf(mm(d4b, d4bd)))
    x = [eye - y for y in d]
    x = add(x, mm(bf(x), d2bd))
    x = add(x, mm(bf(x), d4bd))
    td = add(x, mm(bf(x), d8bd))
    tdm = [y - eye for y in td]
    tdmb = bf(tdm)
    n = add(o, mm(tdmb, bd(bf(o))))
    nb = bf(n)
    n2bd = bd(bf(mm(nb, bd(nb))))
    tnm = [m - y for m, y in zip(mm(bf([eye - y for y in n]), n2bd), n)]
    cross = mm(bf(tnm), bd(tdmb))
    return [p + q + r for p, q, r in zip(tdm, tnm, cross)]


def _gdn_prep(seq, first, qkv_ref, sm_ref, cw_ref, dtb_ref, alog_ref, eb2_ref, eg2_ref, eq2_ref,
              ltri_ref, shift_ref, tail_ref):
    qkv = _conv_silu(qkv_ref[seq], tail_ref, seq, shift_ref, cw_ref, None, first)
    q = qkv[:, 0:DN_WIDTH]
    k = qkv[:, DN_WIDTH:2 * DN_WIDTH]
    v = qkv[:, 2 * DN_WIDTH:3 * DN_WIDTH]
    q = _head_norm(q, DN_HEAD_DIM, 1.0) * (DN_HEAD_DIM ** -0.5)
    k = _head_norm(k, DN_HEAD_DIM, 1.0)

    sm = sm_ref[seq]
    beta_s = jax.nn.sigmoid(sm)
    g_s = -jnp.exp(alog_ref[...]) * _softplus(sm + dtb_ref[...])
    gcum = _dot2l(ltri_ref[...], g_s)
    g_t = gcum.T[16:24, :]
    grow = _chunk_pairs(g_t, pltpu.roll(g_t, 7, 0))
    gcol = _dot2k(gcum, eq2_ref[...])
    beta_x = _dot2k(beta_s, eb2_ref[...])
    gc_x = _dot2k(gcum, eg2_ref[...])
    egc = jnp.exp(gc_x)
    kb = k * beta_x
    return dict(q=q, k=k, kb=kb, vb=v * beta_x, kbg=kb * egc, qd=q * egc, gc_x=gc_x, gcol=gcol, grow=grow)


def _gdn_kernel(qkv_ref, gate_ref, sm_ref, cw_ref, dtb_ref, alog_ref, ng_ref, eb2_ref, eg2_ref,
                eq2_ref, ltri_ref, shift_ref, o_ref, tail_ref, st_ref,
                *, nseq):
    first = pl.program_id(1) == 0

    @pl.when(first)
    def _():
        st_ref[...] = jnp.zeros_like(st_ref)

    prep = [_gdn_prep(seq, first, qkv_ref, sm_ref, cw_ref, dtb_ref, alog_ref, eb2_ref, eg2_ref,
                      eq2_ref, ltri_ref, shift_ref, tail_ref)
            for seq in range(nseq)]

    qw = DN_HEADS * CHUNK
    rp = lax.broadcasted_iota(jnp.int32, (CHUNK, LANES), 0)
    lp = lax.broadcasted_iota(jnp.int32, (CHUNK, LANES), 1)
    cp = lp & (CHUNK - 1)
    tri = rp >= cp
    stri = rp > cp
    blk = (rp >> 4) == (cp >> 4)
    eye = jnp.where(rp == cp, 1.0, 0.0).astype(F32)
    grp_p = [lp < CHUNK, lp >= CHUNK]
    lpw = lax.broadcasted_iota(jnp.int32, (CHUNK, PAIR_W), 1)
    grp_pw = [lpw < DN_HEAD_DIM, lpw >= DN_HEAD_DIM]

    chunks = [slice(ci * CHUNK, (ci + 1) * CHUNK) for ci in range(CPS)]
    pairs = range(DN_HEADS // 2)
    units, a_list, attn_list = [], [], []
    for seq in range(nseq):
        pr = prep[seq]
        for ci, rs in enumerate(chunks):
            for p in pairs:
                ps = slice(p * PAIR_W, (p + 1) * PAIR_W)
                kstack = _row_stack_masked(pr["k"][rs, ps].astype(BF16), grp_pw)
                lhs = jnp.concatenate([pr["kb"][rs, ps], pr["q"][rs, ps]], axis=0).astype(BF16)
                kq = _dot_nt(lhs, kstack)
                g_row = pr["grow"][ci][DNA_COL - 16 + 2 * p:DNA_COL - 15 + 2 * p, :]
                diff = pr["gcol"][rs, p * LANES:(p + 1) * LANES] - g_row
                dm = jnp.exp(jnp.where(tri, diff, NEG))
                units.append((seq, ci, p))
                a_list.append(jnp.where(stri, kq[0:CHUNK] * dm, 0.0))
                attn_list.append((kq[CHUNK:2 * CHUNK] * dm).astype(BF16))
    tm_list = _tri_inv_minus_eye(a_list, eye, blk, grp_p)
    u_list, w_list = [], []
    for (seq, ci, p), tm in zip(units, tm_list):
        rs = chunks[ci]
        ps = slice(p * PAIR_W, (p + 1) * PAIR_W)
        tmb = tm.astype(BF16)
        vb_p = prep[seq]["vb"][rs, ps]
        kbg_p = prep[seq]["kbg"][rs, ps]
        u_list.append(vb_p + _dot(tmb, _row_stack_masked(vb_p.astype(BF16), grp_pw)))
        w_list.append(kbg_p + _dot(tmb, _row_stack_masked(kbg_p.astype(BF16), grp_pw)))
    unit_of = {u: i for i, u in enumerate(units)}

    outs = [[] for _ in range(nseq)]
    for ci, rs in enumerate(chunks):
        for seq in range(nseq):
            pr = prep[seq]
            gx = pr["gc_x"][rs]
            glast = gx[CHUNK - 1:CHUNK, :]
            kd = pr["k"][rs] * jnp.exp(glast - gx)
            sdec = jnp.exp(glast)
            out_p = []
            for p in pairs:
                i = unit_of[(seq, ci, p)]
                wq = []
                for hh in range(2):
                    h = 2 * p + hh
                    th = slice(h * DN_HEAD_DIM, (h + 1) * DN_HEAD_DIM)
                    tp = slice(hh * DN_HEAD_DIM, (hh + 1) * DN_HEAD_DIM)
                    lhs = jnp.concatenate([w_list[i][:, tp], pr["qd"][rs, th]], axis=0).astype(BF16)
                    wq.append(_dot(lhs, st_ref[seq, h].astype(BF16)))
                vn = u_list[i] - jnp.concatenate([x[0:CHUNK] for x in wq], axis=1)
                vnb = vn.astype(BF16)
                ha, hb = 2 * p, 2 * p + 1
                kd_t = jnp.concatenate([kd[:, ha * DN_HEAD_DIM:(ha + 1) * DN_HEAD_DIM],
                                        kd[:, hb * DN_HEAD_DIM:(hb + 1) * DN_HEAD_DIM]], axis=0).T
                lhs = jnp.concatenate([attn_list[i], kd_t.astype(BF16)], axis=0)
                res = _dot(lhs, _row_stack_masked(vnb, grp_pw))
                out_p.append(jnp.concatenate([x[CHUNK:2 * CHUNK] for x in wq], axis=1) + res[0:CHUNK])
                for hh in range(2):
                    h = 2 * p + hh
                    th = slice(h * DN_HEAD_DIM, (h + 1) * DN_HEAD_DIM)
                    tp = slice(hh * DN_HEAD_DIM, (hh + 1) * DN_HEAD_DIM)
                    st_ref[seq, h] = st_ref[seq, h] * sdec[:, th] + res[CHUNK:CHUNK + DN_HEAD_DIM, tp]
            outs[seq].append(jnp.concatenate(out_p, axis=1))
    for seq in range(nseq):
        o = jnp.concatenate(outs[seq], axis=0)
        o = _head_norm(o, DN_HEAD_DIM, 1.0 / DN_HEAD_DIM) * ng_ref[...]
        o_ref[seq] = (o * _silu(gate_ref[seq].astype(F32))).astype(BF16)


def _gdn(main, small, nb, ns, cw, dtb, alog, ng, consts):
    nseq = _nseq(nb, GDN_NSEQ)
    spb = ns // ROWS
    main3 = main.reshape(nb, ns, MAIN_WIDTH)
    small3 = small.reshape(nb, ns, SMALL_WIDTH)
    rowmap = lambda blk: (lambda b, c: (b, c, blk))
    const = lambda shp: pl.BlockSpec(shp, lambda b, c: (0,) * len(shp))
    qw = DN_HEADS * CHUNK
    y = pl.pallas_call(
        functools.partial(_gdn_kernel, nseq=nseq),
        grid=(nb // nseq, spb),
        in_specs=[
            pl.BlockSpec((nseq, ROWS, DN_CONV_DIM), rowmap(DNQKV_BLK)),
            pl.BlockSpec((nseq, ROWS, DN_WIDTH), rowmap(DNGATE_BLK)),
            pl.BlockSpec((nseq, ROWS, SMALL_WIDTH), rowmap(0)),
            const((CONV_K, DN_CONV_DIM)),
            const((1, SMALL_WIDTH)), const((1, SMALL_WIDTH)), const((1, DN_WIDTH)),
            const((2 * SMALL_WIDTH, DN_WIDTH)), const((2 * SMALL_WIDTH, DN_WIDTH)),
            const((2 * SMALL_WIDTH, qw)),
            const((ROWS, ROWS)),
            const(((CONV_K - 1) * ROWS, ROWS)),
        ],
        out_specs=pl.BlockSpec((nseq, ROWS, DN_WIDTH), rowmap(0)),
        out_shape=jax.ShapeDtypeStruct((nb, ns, DN_WIDTH), BF16),
        scratch_shapes=[pltpu.VMEM((nseq, TAIL, DN_CONV_DIM), F32),
                        pltpu.VMEM((nseq, DN_HEADS, DN_HEAD_DIM, DN_HEAD_DIM), F32)],
        compiler_params=pltpu.CompilerParams(
            dimension_semantics=("arbitrary", "arbitrary"), vmem_limit_bytes=VMEM_LIMIT),
        name="gdn",
    )(main3, main3, small3, cw, dtb, alog, ng,
      consts["e_beta2"], consts["e_g2"], consts["e_gq2"], consts["ltri"],
      consts["shift"])
    return y.reshape(nb * ns, DN_WIDTH)


_HEAD_PERM = np.concatenate([np.arange(0, SSD_HEADS, 2), np.arange(1, SSD_HEADS, 2)])


def _constants():
    r = np.arange(ROWS)
    same_chunk = (r[:, None] // CHUNK) == (r[None, :] // CHUNK)
    ltri = (same_chunk & (r[None, :] <= r[:, None])).astype(np.float32)
    e_ssd = np.zeros((SMALL_WIDTH, SSD_WIDTH), np.float32)
    for col, h in enumerate(_HEAD_PERM):
        e_ssd[col, h * SSD_HEAD_DIM:(h + 1) * SSD_HEAD_DIM] = 1.0
    qw = DN_HEADS * CHUNK
    e_beta = np.zeros((SMALL_WIDTH, DN_WIDTH), np.float32)
    e_g = np.zeros((SMALL_WIDTH, DN_WIDTH), np.float32)
    e_gq = np.zeros((SMALL_WIDTH, qw), np.float32)
    for h in range(DN_HEADS):
        e_beta[DNB_COL + h, h * DN_HEAD_DIM:(h + 1) * DN_HEAD_DIM] = 1.0
        e_g[DNA_COL + h, h * DN_HEAD_DIM:(h + 1) * DN_HEAD_DIM] = 1.0
        e_gq[DNA_COL + h, h * CHUNK:(h + 1) * CHUNK] = 1.0
    lane = np.arange(ATTN_WIDTH)
    bd64 = ((lane[:, None] // ATTN_HEAD_DIM) == (lane[None, :] // ATTN_HEAD_DIM)).astype(np.float32)
    bd64 = bd64 / ATTN_HEAD_DIM
    shift = np.zeros(((CONV_K - 1) * ROWS, ROWS), np.float32)
    for tap in range(CONV_K - 1):
        s = CONV_K - 1 - tap
        shift[tap * ROWS + np.arange(s, ROWS), np.arange(0, ROWS - s)] = 1.0
    twice = lambda m: np.concatenate([m, m], axis=0)
    out = dict(ltri=ltri, e_ssd2=twice(e_ssd), e_beta2=twice(e_beta), e_g2=twice(e_g),
               e_gq2=twice(e_gq), bd64=bd64, shift=shift)
    return {k: jnp.asarray(v, BF16) for k, v in out.items()}


def _attn_bias(rel_bias):
    r = np.arange(ATTN_TQ)[:, None]
    c = np.arange(ATTN_NK)[None, :]
    qc, kc = r // CHUNK, c // CHUNK
    valid = (kc >= qc) & (kc <= qc + ATTN_BAND_CHUNKS - 1)
    nh = rel_bias.shape[0]
    assert ATTN_TQ - 1 <= ATTN_REL_CLIP
    near = rel_bias[:, ATTN_REL_CLIP - (ATTN_TQ - 1):2 * ATTN_REL_CLIP]
    far = jnp.broadcast_to(rel_bias[:, 2 * ATTN_REL_CLIP:], (nh, ATTN_NK - ATTN_REL_CLIP))
    t = jnp.concatenate([near, far], axis=1)
    span = ATTN_TQ + ATTN_NK - 1
    w = jnp.concatenate([t[:, ::-1], jnp.zeros((nh, 1), rel_bias.dtype)], axis=1)
    skew = jnp.tile(w, (1, ATTN_TQ))[:, :ATTN_TQ * span].reshape(nh, ATTN_TQ, span)
    bias = skew[:, :, ATTN_TQ - 1:ATTN_TQ - 1 + ATTN_NK]
    bias = jnp.where(valid[None], bias, NEG).astype(F32)
    ext = jnp.concatenate([bias, jnp.full((nh, ATTN_TQ, ATTN_PAD), NEG, F32)], axis=2)
    offs = [max(ATTN_PAD - v * ATTN_TQ, 0) for v in range(ATTN_PAD // ATTN_TQ + 1)]
    return jnp.stack([ext[:, :, o:o + ATTN_NK] for o in offs], axis=0)


def _pad_row(v, start):
    return jnp.zeros((1, SMALL_WIDTH), F32).at[0, start:start + v.shape[0]].set(v.astype(F32))


def kernel(x, norm_g, w_in, ssd_conv_w, ssd_conv_b, ssd_dt_bias, ssd_a_log, ssd_d, ssd_norm_g,
           attn_q_norm_g, attn_k_norm_g, attn_rel_bias, dn_conv_w, dn_dt_bias, dn_a_log,
           dn_norm_g, w_out):
    nb, ns, _ = x.shape
    depth = w_in.shape[0]
    assert ns % ROWS == 0 and ns % ATTN_PAD == 0
    consts = _constants()
    x2 = x.reshape(nb * ns, D_MODEL)
    w_main = jnp.concatenate(
        [w_in[:, :, 1024:2560], w_in[:, :, 4624:6160], w_in[:, :, 0:1024], w_in[:, :, 2576:4624],
         w_in[:, :, 6160:6672]], axis=2).astype(BF16)
    w_small = jnp.concatenate(
        [w_in[:, :, 2560:2576][:, :, _HEAD_PERM], w_in[:, :, 6672:6680],
         jnp.zeros((depth, D_MODEL, SMALL_WIDTH - 24), F32)], axis=2).astype(BF16)
    wo = w_out.astype(BF16)
    pad_rows = lambda v, start: jnp.zeros((depth, 1, SMALL_WIDTH), F32).at[
        :, 0, start:start + v.shape[1]].set(v.astype(F32))
    ssd_dtb = pad_rows(ssd_dt_bias[:, _HEAD_PERM], 0)
    ssd_alog = pad_rows(ssd_a_log[:, _HEAD_PERM], 0)
    ssd_dexp = jnp.repeat(ssd_d.astype(F32), SSD_HEAD_DIM, axis=1)
    dn_dtb = pad_rows(dn_dt_bias, DNA_COL)
    dn_alog = pad_rows(dn_a_log, DNA_COL)
    gq = jnp.tile(attn_q_norm_g, (1, ATTN_HEADS))
    gk = jnp.tile(attn_k_norm_g, (1, ATTN_HEADS))
    dn_ng = jnp.tile(dn_norm_g, (1, DN_HEADS))
    bias = _attn_bias(attn_rel_bias.reshape(depth * ATTN_HEADS, -1))
    for l in range(depth):
        main, small, qn, kn, vn = _inproj(
            x2, norm_g[l][None, :], w_main, w_small, l, gq[l][None, :], gk[l][None, :], consts["bd64"])
        y_ssd = _ssd(main, small, nb, ns, ssd_conv_w[l], ssd_conv_b[l][None, :], ssd_dtb[l], ssd_alog[l],
                     ssd_dexp[l][None, :], ssd_norm_g[l][None, :], consts)
        y_attn = _attn(qn, kn, vn, main, nb, ns, bias, l)
        y_dn = _gdn(main, small, nb, ns, dn_conv_w[l], dn_dtb[l], dn_alog[l], dn_ng[l][None, :], consts)
        x2 = _outproj(x2, y_ssd, y_attn, y_dn, wo, l)
    return x2.reshape(nb, ns, D_MODEL)
```

```python
import functools

import numpy as np
import jax
import jax.numpy as jnp
from jax import lax
from jax.experimental import pallas as pl
from jax.experimental.pallas import tpu as pltpu

F32 = jnp.float32
BF16 = jnp.bfloat16

D_MODEL = 1024
CHUNK = 64
EPS = 1e-6
CONV_K = 4
SSD_WIDTH = 1024
SSD_HEADS = 16
SSD_HEAD_DIM = 64
SSD_GROUPS = 2
SSD_STATE = 128
SSD_CONV_DIM = 1536
ATTN_WIDTH = 512
ATTN_HEADS = 8
ATTN_HEAD_DIM = 64
ATTN_BAND_CHUNKS = 9
ATTN_REL_CLIP = 128
DN_WIDTH = 512
DN_HEADS = 4
DN_HEAD_DIM = 128
DN_CONV_DIM = 1536
MIX_WIDTH = 2048

MAIN_WIDTH = 6656
XBC_BLK = 0
DNQKV_BLK = 1
Z_BLK = 3
AQ_BLK, AK_BLK, AV_BLK, AGATE_BLK = 8, 9, 10, 11
DNGATE_BLK = 12
SMALL_WIDTH = 128
DNB_COL = 16
DNA_COL = 20

LANES = 128
NEG = -1e30
VMEM_LIMIT = 56 * 1024 * 1024

ROWS = 256
CPS = ROWS // CHUNK
TAIL = 8
ATTN_TQ = 128
ATTN_NK = ATTN_TQ + (ATTN_BAND_CHUNKS - 1) * CHUNK
ATTN_PAD = (ATTN_BAND_CHUNKS - 1) * CHUNK
PAIR_W = 2 * DN_HEAD_DIM
GDN_NSEQ = 4
SSD_NSEQ = 4


def _dot(a, b):
    return jnp.dot(a, b, preferred_element_type=F32)


def _dot_nt(a, b):
    return lax.dot_general(a, b, (((1,), (1,)), ((), ())), preferred_element_type=F32)


def _split(v):
    hi = v.astype(BF16)
    lo = (v - hi.astype(F32)).astype(BF16)
    return hi, lo


def _dot2(v, m):
    hi, lo = _split(v)
    return _dot(hi, m) + _dot(lo, m)


def _dot2k(v, m2):
    hi, lo = _split(v)
    return _dot(jnp.concatenate([hi, lo], axis=1), m2)


def _dot2l(m, v):
    hi, lo = _split(v)
    return _dot(m, hi) + _dot(m, lo)


def _silu(x):
    return x * jax.nn.sigmoid(x)


def _softplus(x):
    return jnp.maximum(x, 0.0) + jnp.log1p(jnp.exp(-jnp.abs(x)))


def _conv_silu(xb, tail_ref, seq, shift_ref, cw_ref, bias, first):
    rows = xb.shape[0]

    @pl.when(first)
    def _():
        tail_ref[seq] = jnp.zeros(tail_ref.shape[1:], tail_ref.dtype)

    tail = tail_ref[seq]
    xf = xb.astype(F32)
    acc = cw_ref[CONV_K - 1:CONV_K, :] * xf
    row = lax.broadcasted_iota(jnp.int32, tail.shape, 0)
    shifted = _dot(shift_ref[...], xb)
    head = None
    for t in range(CONV_K - 1):
        s = CONV_K - 1 - t
        w_t = cw_ref[t:t + 1, :]
        acc = acc + w_t * shifted[t * rows:(t + 1) * rows]
        corr = w_t * jnp.where(row < s, pltpu.roll(tail, s, 0), 0.0)
        head = corr if head is None else head + corr
    acc = jnp.concatenate([acc[0:TAIL] + head, acc[TAIL:]], axis=0)
    if bias is not None:
        acc = acc + bias
    tail_ref[seq] = xf[rows - TAIL:rows, :]
    return _silu(acc)


def _head_norm(x, width, scale):
    parts = []
    for h in range(x.shape[1] // width):
        xh = x[:, h * width:(h + 1) * width]
        ss = jnp.sum(xh * xh, axis=-1, keepdims=True)
        parts.append(xh * lax.rsqrt(ss * scale + EPS))
    return jnp.concatenate(parts, axis=1)


def _chunk_pairs(lo, hi):
    lane_lo = lax.broadcasted_iota(jnp.int32, (8, LANES), 1) < CHUNK
    tile = lambda x, k: x[:, k * LANES:(k + 1) * LANES]
    swap = lambda x: pltpu.roll(x, CHUNK, 1)
    out = []
    for ci in range(CPS):
        k = ci // 2
        if ci % 2 == 0:
            out.append(jnp.where(lane_lo, tile(lo, k), swap(tile(hi, k))))
        else:
            out.append(jnp.where(lane_lo, swap(tile(lo, k)), tile(hi, k)))
    return out


def _nseq(nb, most=2):
    n = most
    while nb % n:
        n //= 2
    return n


INPROJ_TN = MAIN_WIDTH // 2
ATTN_COL0 = AQ_BLK * ATTN_WIDTH - INPROJ_TN


def _inproj_kernel(x_ref, g_ref, w_ref, ws_ref, gq_ref, gk_ref, bd_ref,
                   main_ref, small_ref, qn_ref, kn_ref, vn_ref, h_ref):
    j = pl.program_id(1)

    @pl.when(j == 0)
    def _():
        x = x_ref[...]
        ms = jnp.mean(x * x, axis=-1, keepdims=True)
        h = (x * lax.rsqrt(ms + EPS) * g_ref[...]).astype(BF16)
        h_ref[...] = h
        small_ref[...] = _dot(h, ws_ref[...])

    acc = _dot(h_ref[...], w_ref[...])
    main_ref[...] = acc.astype(BF16)

    @pl.when(j == 1)
    def _():
        bd = bd_ref[...]
        q = acc[:, ATTN_COL0:ATTN_COL0 + ATTN_WIDTH]
        k = acc[:, ATTN_COL0 + ATTN_WIDTH:ATTN_COL0 + 2 * ATTN_WIDTH]
        qms = _dot((q * q).astype(BF16), bd)
        kms = _dot((k * k).astype(BF16), bd)
        qn_ref[...] = (q * lax.rsqrt(qms + EPS) * gq_ref[...] * (ATTN_HEAD_DIM ** -0.5)).astype(BF16)
        kn_ref[...] = (k * lax.rsqrt(kms + EPS) * gk_ref[...]).astype(BF16)
        vn_ref[...] = acc[:, ATTN_COL0 + 2 * ATTN_WIDTH:ATTN_COL0 + 3 * ATTN_WIDTH].astype(BF16)


def _inproj(x2, g, w_main, w_small, layer, gq, gk, bd64):
    t = x2.shape[0]
    tm = min(1024, t)
    assert MAIN_WIDTH // INPROJ_TN == 2
    const = lambda shp: pl.BlockSpec(shp, lambda i, j: (0, 0))
    qkv_spec = pl.BlockSpec((tm, ATTN_WIDTH), lambda i, j: (i, 0))
    qkv_shape = jax.ShapeDtypeStruct((t, ATTN_WIDTH), BF16)
    return pl.pallas_call(
        _inproj_kernel,
        grid=(t // tm, 2),
        in_specs=[
            pl.BlockSpec((tm, D_MODEL), lambda i, j: (i, 0)),
            const((1, D_MODEL)),
            pl.BlockSpec((None, D_MODEL, INPROJ_TN), lambda i, j: (layer, 0, j)),
            pl.BlockSpec((None, D_MODEL, SMALL_WIDTH), lambda i, j: (layer, 0, 0)),
            const((1, ATTN_WIDTH)), const((1, ATTN_WIDTH)), const((ATTN_WIDTH, ATTN_WIDTH)),
        ],
        out_specs=[
            pl.BlockSpec((tm, INPROJ_TN), lambda i, j: (i, j)),
            pl.BlockSpec((tm, SMALL_WIDTH), lambda i, j: (i, 0)),
            qkv_spec, qkv_spec, qkv_spec,
        ],
        out_shape=[
            jax.ShapeDtypeStruct((t, MAIN_WIDTH), BF16),
            jax.ShapeDtypeStruct((t, SMALL_WIDTH), F32),
            qkv_shape, qkv_shape, qkv_shape,
        ],
        scratch_shapes=[pltpu.VMEM((tm, D_MODEL), BF16)],
        compiler_params=pltpu.CompilerParams(
            dimension_semantics=("arbitrary", "arbitrary"), vmem_limit_bytes=VMEM_LIMIT),
        name="inproj",
    )(x2, g, w_main, w_small, gq, gk, bd64)


def _outproj_kernel(x_ref, ys_ref, ya_ref, yd_ref, ws_ref, wa_ref, wd_ref, o_ref):
    acc = _dot(ys_ref[...], ws_ref[...])
    acc = acc + _dot(ya_ref[...], wa_ref[...])
    acc = acc + _dot(yd_ref[...], wd_ref[...])
    o_ref[...] = x_ref[...] + acc


def _outproj(x2, y_ssd, y_attn, y_dn, wo, layer):
    t = x2.shape[0]
    tm = min(1024, t)
    row = lambda w: pl.BlockSpec((tm, w), lambda i: (i, 0))
    slab = lambda r, blk: pl.BlockSpec((None, r, D_MODEL), lambda i: (layer, blk, 0))
    assert SSD_WIDTH % ATTN_WIDTH == 0 and ATTN_WIDTH == DN_WIDTH
    return pl.pallas_call(
        _outproj_kernel,
        grid=(t // tm,),
        in_specs=[row(D_MODEL), row(SSD_WIDTH), row(ATTN_WIDTH), row(DN_WIDTH),
                  slab(SSD_WIDTH, 0), slab(ATTN_WIDTH, SSD_WIDTH // ATTN_WIDTH),
                  slab(DN_WIDTH, SSD_WIDTH // ATTN_WIDTH + 1)],
        out_specs=row(D_MODEL),
        out_shape=jax.ShapeDtypeStruct((t, D_MODEL), F32),
        compiler_params=pltpu.CompilerParams(
            dimension_semantics=("arbitrary",), vmem_limit_bytes=VMEM_LIMIT),
        name="outproj",
    )(x2, y_ssd, y_attn, y_dn, wo, wo, wo)


def _ssd_prep(seq, first, xbc_ref, sm_ref, cw_ref, cb_ref, dtb_ref, alog_ref, e2_ref, ltri_ref,
              shift_ref, tail_ref):
    xa = _conv_silu(xbc_ref[seq], tail_ref, seq, shift_ref, cw_ref, cb_ref[...], first)
    xs = xa[:, 0:SSD_WIDTH]
    bm = xa[:, SSD_WIDTH:SSD_WIDTH + 256]
    cm = xa[:, SSD_WIDTH + 256:SSD_WIDTH + 512]

    dt = _softplus(sm_ref[seq] + dtb_ref[...])
    d_a = dt * (-jnp.exp(alog_ref[...]))
    acum = _dot2l(ltri_ref[...], d_a)
    acum_t = acum.T
    prow = _chunk_pairs(acum_t[0:8, :], acum_t[8:16, :])

    e2 = e2_ref[...]
    acum_x = _dot2k(acum, e2)
    dt_x = _dot2k(dt, e2)
    return dict(xs=xs, bm=bm, cm=cm, prow=prow, acum_x=acum_x, xdt=xs * dt_x, eacum=jnp.exp(acum_x))


def _ssd_chunk(pr, seq, ci, st_ref, dexp_ref, tri2, lane_lo):
    rs = slice(ci * CHUNK, (ci + 1) * CHUNK)
    ax = pr["acum_x"][rs]
    last = ax[CHUNK - 1:CHUNK, :]
    dec = jnp.exp(last - ax)
    cdec = jnp.exp(last)
    xdt_c = pr["xdt"][rs]
    xdd = (xdt_c * dec).astype(BF16)
    xdt_b = xdt_c.astype(BF16)
    pc = pr["prow"][ci]
    ydiag, yoff = [], []
    for g in range(SSD_GROUPS):
        gs = slice(g * 512, (g + 1) * 512)
        bg = pr["bm"][rs, g * SSD_STATE:(g + 1) * SSD_STATE]
        cg = pr["cm"][rs, g * SSD_STATE:(g + 1) * SSD_STATE].astype(BF16)
        bgt2 = jnp.concatenate([bg, bg], axis=0).T.astype(BF16)
        cb2 = _dot(cg, bgt2)
        st_g = st_ref[seq, :, gs]
        yoff.append(_dot(cg, st_g.astype(BF16)))
        new_g = _dot(bgt2[:, 0:CHUNK], xdd[:, gs])
        st_ref[seq, :, gs] = st_g * cdec[:, gs] + new_g
        for jp in range(4):
            j = g * 4 + jp
            tl = slice(j * LANES, (j + 1) * LANES)
            diff = ax[:, tl] - pc[j:j + 1, :]
            lp = jnp.exp(jnp.where(tri2, diff, NEG))
            mp = (cb2 * lp).astype(BF16)
            xt = xdt_b[:, tl]
            zx = jnp.zeros_like(xt)
            xbd = jnp.concatenate([jnp.where(lane_lo, xt, zx), jnp.where(lane_lo, zx, xt)],
                                  axis=0)
            ydiag.append(_dot(mp, xbd))
    return (jnp.concatenate(ydiag, axis=1) + jnp.concatenate(yoff, axis=1) * pr["eacum"][rs]
            + pr["xs"][rs] * dexp_ref[...])


def _ssd_kernel(*refs, nseq):
    for stage in _ssd_stages(*refs, nseq=nseq):
        stage()


def _ssd_stages(xbc_ref, z_ref, sm_ref, cw_ref, cb_ref, dtb_ref, alog_ref, dexp_ref, ng_ref,
                e2_ref, ltri_ref, shift_ref, y_ref, tail_ref, st_ref, *, nseq):
    first = pl.program_id(1) == 0

    @pl.when(first)
    def _():
        st_ref[...] = jnp.zeros_like(st_ref)

    row_i = lax.broadcasted_iota(jnp.int32, (CHUNK, LANES), 0)
    lane = lax.broadcasted_iota(jnp.int32, (CHUNK, LANES), 1)
    lane_lo = lane < CHUNK
    tri2 = row_i >= jnp.where(lane_lo, lane, lane - CHUNK)

    prep = [_ssd_prep(seq, first, xbc_ref, sm_ref, cw_ref, cb_ref, dtb_ref, alog_ref, e2_ref,
                      ltri_ref, shift_ref, tail_ref) for seq in range(nseq)]
    ys = [[] for _ in range(nseq)]

    def chunk_round(ci):
        for seq in range(nseq):
            ys[seq].append(_ssd_chunk(prep[seq], seq, ci, st_ref, dexp_ref, tri2, lane_lo))

    def finish():
        for seq in range(nseq):
            y = jnp.concatenate(ys[seq], axis=0) * _silu(z_ref[seq].astype(F32))
            outs = []
            for g in range(SSD_GROUPS):
                gs = slice(g * 512, (g + 1) * 512)
                yg = y[:, gs]
                ms = jnp.mean(yg * yg, axis=-1, keepdims=True)
                outs.append(yg * lax.rsqrt(ms + EPS) * ng_ref[:, gs])
            y_ref[seq] = jnp.concatenate(outs, axis=1).astype(BF16)

    return [functools.partial(chunk_round, ci) for ci in range(CPS)] + [finish]


def _ssd(main, small, nb, ns, cw, cb, dtb, alog, dexp, ng, consts):
    nseq = _nseq(nb, SSD_NSEQ)
    spb = ns // ROWS
    main3 = main.reshape(nb, ns, MAIN_WIDTH)
    small3 = small.reshape(nb, ns, SMALL_WIDTH)
    rowmap = lambda blk: (lambda b, c: (b, c, blk))
    const = lambda shp: pl.BlockSpec(shp, lambda b, c: (0,) * len(shp))
    y = pl.pallas_call(
        functools.partial(_ssd_kernel, nseq=nseq),
        grid=(nb // nseq, spb),
        in_specs=[
            pl.BlockSpec((nseq, ROWS, SSD_CONV_DIM), rowmap(XBC_BLK)),
            pl.BlockSpec((nseq, ROWS, SSD_WIDTH), rowmap(Z_BLK)),
            pl.BlockSpec((nseq, ROWS, SMALL_WIDTH), rowmap(0)),
            const((CONV_K, SSD_CONV_DIM)), const((1, SSD_CONV_DIM)),
            const((1, SMALL_WIDTH)), const((1, SMALL_WIDTH)),
            const((1, SSD_WIDTH)), const((1, SSD_WIDTH)),
            const((2 * SMALL_WIDTH, SSD_WIDTH)), const((ROWS, ROWS)),
            const(((CONV_K - 1) * ROWS, ROWS)),
        ],
        out_specs=pl.BlockSpec((nseq, ROWS, SSD_WIDTH), rowmap(0)),
        out_shape=jax.ShapeDtypeStruct((nb, ns, SSD_WIDTH), BF16),
        scratch_shapes=[pltpu.VMEM((nseq, TAIL, SSD_CONV_DIM), F32),
                        pltpu.VMEM((nseq, SSD_STATE, SSD_WIDTH), F32)],
        compiler_params=pltpu.CompilerParams(
            dimension_semantics=("arbitrary", "arbitrary"), vmem_limit_bytes=VMEM_LIMIT),
        name="ssd",
    )(main3, main3, small3, cw, cb, dtb, alog, dexp, ng,
      consts["e_ssd2"], consts["ltri"], consts["shift"])
    return y.reshape(nb * ns, SSD_WIDTH)


def _attn_kernel(q_ref, k_ref, v_ref, gate_ref, bias_ref, o_ref, *, nseq):
    qi = pl.program_id(1)
    start = pl.multiple_of(jnp.maximum(qi * ATTN_TQ - ATTN_PAD, 0), ATTN_TQ)
    lane_lo = lax.broadcasted_iota(jnp.int32, (ATTN_TQ, LANES), 1) < ATTN_HEAD_DIM
    lane_hi = jnp.logical_not(lane_lo)
    zq = jnp.zeros((ATTN_TQ, LANES), BF16)
    tile = lambda x, h: x[:, (h // 2) * LANES:(h // 2 + 1) * LANES]
    units = [(s, h) for s in range(nseq) for h in range(ATTN_HEADS)]
    q = [q_ref[s] for s in range(nseq)]
    kk = [k_ref[s, pl.ds(start, ATTN_NK), :] for s in range(nseq)]
    vv = [v_ref[s, pl.ds(start, ATTN_NK), :] for s in range(nseq)]
    qm = [jnp.where(lane_lo if h % 2 == 0 else lane_hi, tile(q[s], h), zq) for s, h in units]
    sc = [_dot_nt(qm[u], tile(kk[s], h)) + bias_ref[0, h] for u, (s, h) in enumerate(units)]
    mx = [jnp.max(x, axis=-1, keepdims=True) for x in sc]
    p = [jnp.exp(x - m) for x, m in zip(sc, mx)]
    den = [jnp.sum(x, axis=-1, keepdims=True) for x in p]
    ov = [_dot(p[u].astype(BF16), tile(vv[s], h)) / den[u] for u, (s, h) in enumerate(units)]
    for s in range(nseq):
        base = s * ATTN_HEADS
        outs = [jnp.where(lane_lo, ov[base + 2 * j], ov[base + 2 * j + 1]) for j in range(ATTN_HEADS // 2)]
        o = jnp.concatenate(outs, axis=1) * _silu(gate_ref[s].astype(F32))
        o_ref[s] = o.astype(BF16)


def _attn(qn, kn, vn, main, nb, ns, bias, layer):
    nseq = _nseq(nb)
    spb = ns // ATTN_TQ
    assert ns >= ATTN_NK
    nvar = bias.shape[0]
    blk = lambda rows, cb: pl.BlockSpec((nseq, rows, ATTN_WIDTH), cb)
    y = pl.pallas_call(
        functools.partial(_attn_kernel, nseq=nseq),
        grid=(nb // nseq, spb),
        in_specs=[
            blk(ATTN_TQ, lambda b, i: (b, i, 0)),
            blk(ns, lambda b, i: (b, 0, 0)),
            blk(ns, lambda b, i: (b, 0, 0)),
            blk(ATTN_TQ, lambda b, i: (b, i, AGATE_BLK)),
            pl.BlockSpec((1, ATTN_HEADS, ATTN_TQ, ATTN_NK),
                         lambda b, i: (jnp.minimum(i, nvar - 1), layer, 0, 0)),
        ],
        out_specs=blk(ATTN_TQ, lambda b, i: (b, i, 0)),
        out_shape=jax.ShapeDtypeStruct((nb, ns, ATTN_WIDTH), BF16),
        compiler_params=pltpu.CompilerParams(
            dimension_semantics=("arbitrary", "arbitrary"), vmem_limit_bytes=VMEM_LIMIT),
        name="attn",
    )(qn.reshape(nb, ns, ATTN_WIDTH), kn.reshape(nb, ns, ATTN_WIDTH),
      vn.reshape(nb, ns, ATTN_WIDTH), main.reshape(nb, ns, MAIN_WIDTH), bias)
    return y.reshape(nb * ns, ATTN_WIDTH)


def _row_stack_masked(y, groups):
    z = jnp.zeros_like(y)
    return jnp.concatenate([jnp.where(g, y, z) for g in groups], axis=0)


def _tri_inv_minus_eye(a_list, eye, blk, grp, hook=lambda: None):
    bf = lambda xs: [x.astype(BF16) for x in xs]
    bd = lambda xs: [_row_stack_masked(x, grp) for x in xs]
    mm = lambda xs, ys: [_dot(x, y) for x, y in zip(xs, ys)]
    add = lambda xs, ys: [x + y for x, y in zip(xs, ys)]
    d = [jnp.where(blk, a, 0.0) for a in a_list]
    o = [a - x for a, x in zip(a_list, d)]
    db = bf(d)
    d2b = bf(mm(db, bd(db)))
    hook()
    d2bd = bd(d2b)
    d4b = bf(mm(d2b, d2bd))
    hook()
    d4bd = bd(d4b)
    d8bd = bd(bf(mm(d4b, d4bd)))
    x = [eye - y for y in d]
    x = add(x, mm(bf(x), d2bd))
    hook()
    x = add(x, mm(bf(x), d4bd))
    td = add(x, mm(bf(x), d8bd))
    hook()
    tdm = [y - eye for y in td]
    tdmb = bf(tdm)
    n = add(o, mm(tdmb, bd(bf(o))))
    nb = bf(n)
    n2bd = bd(bf(mm(nb, bd(nb))))
    tnm = [m - y for m, y in zip(mm(bf([eye - y for y in n]), n2bd), n)]
    cross = mm(bf(tnm), bd(tdmb))
    return [p + q + r for p, q, r in zip(tdm, tnm, cross)]


def _gdn_prep(seq, first, qkv_ref, sm_ref, cw_ref, dtb_ref, alog_ref, eb2_ref, eg2_ref, eq2_ref,
              ltri_ref, shift_ref, tail_ref):
    qkv = _conv_silu(qkv_ref[seq], tail_ref, seq, shift_ref, cw_ref, None, first)
    q = qkv[:, 0:DN_WIDTH]
    k = qkv[:, DN_WIDTH:2 * DN_WIDTH]
    v = qkv[:, 2 * DN_WIDTH:3 * DN_WIDTH]
    q = _head_norm(q, DN_HEAD_DIM, 1.0) * (DN_HEAD_DIM ** -0.5)
    k = _head_norm(k, DN_HEAD_DIM, 1.0)

    sm = sm_ref[seq]
    beta_s = jax.nn.sigmoid(sm)
    g_s = -jnp.exp(alog_ref[...]) * _softplus(sm + dtb_ref[...])
    gcum = _dot2l(ltri_ref[...], g_s)
    g_t = gcum.T[16:24, :]
    grow = _chunk_pairs(g_t, pltpu.roll(g_t, 7, 0))
    gcol = _dot2k(gcum, eq2_ref[...])
    beta_x = _dot2k(beta_s, eb2_ref[...])
    gc_x = _dot2k(gcum, eg2_ref[...])
    egc = jnp.exp(gc_x)
    kb = k * beta_x
    return dict(q=q, k=k, kb=kb, vb=v * beta_x, kbg=kb * egc, qd=q * egc, gc_x=gc_x, gcol=gcol, grow=grow)


def _gdn_kernel(qkv_ref, gate_ref, sm_ref, cw_ref, dtb_ref, alog_ref, ng_ref, eb2_ref, eg2_ref,
                eq2_ref, ltri_ref, shift_ref, o_ref, tail_ref, st_ref,
                *, nseq, hook=lambda: None):
    first = pl.program_id(1) == 0

    @pl.when(first)
    def _():
        st_ref[...] = jnp.zeros_like(st_ref)

    prep = [_gdn_prep(seq, first, qkv_ref, sm_ref, cw_ref, dtb_ref, alog_ref, eb2_ref, eg2_ref,
                      eq2_ref, ltri_ref, shift_ref, tail_ref)
            for seq in range(nseq)]

    qw = DN_HEADS * CHUNK
    rp = lax.broadcasted_iota(jnp.int32, (CHUNK, LANES), 0)
    lp = lax.broadcasted_iota(jnp.int32, (CHUNK, LANES), 1)
    cp = lp & (CHUNK - 1)
    tri = rp >= cp
    stri = rp > cp
    blk = (rp >> 4) == (cp >> 4)
    eye = jnp.where(rp == cp, 1.0, 0.0).astype(F32)
    grp_p = [lp < CHUNK, lp >= CHUNK]
    lpw = lax.broadcasted_iota(jnp.int32, (CHUNK, PAIR_W), 1)
    grp_pw = [lpw < DN_HEAD_DIM, lpw >= DN_HEAD_DIM]

    chunks = [slice(ci * CHUNK, (ci + 1) * CHUNK) for ci in range(CPS)]
    pairs = range(DN_HEADS // 2)
    units, a_list, attn_list = [], [], []
    for seq in range(nseq):
        pr = prep[seq]
        for ci, rs in enumerate(chunks):
            for p in pairs:
                ps = slice(p * PAIR_W, (p + 1) * PAIR_W)
                kstack = _row_stack_masked(pr["k"][rs, ps].astype(BF16), grp_pw)
                lhs = jnp.concatenate([pr["kb"][rs, ps], pr["q"][rs, ps]], axis=0).astype(BF16)
                kq = _dot_nt(lhs, kstack)
                g_row = pr["grow"][ci][DNA_COL - 16 + 2 * p:DNA_COL - 15 + 2 * p, :]
                diff = pr["gcol"][rs, p * LANES:(p + 1) * LANES] - g_row
                dm = jnp.exp(jnp.where(tri, diff, NEG))
                units.append((seq, ci, p))
                a_list.append(jnp.where(stri, kq[0:CHUNK] * dm, 0.0))
                attn_list.append((kq[CHUNK:2 * CHUNK] * dm).astype(BF16))
    tm_list = _tri_inv_minus_eye(a_list, eye, blk, grp_p, hook)
    u_list, w_list = [], []
    for (seq, ci, p), tm in zip(units, tm_list):
        rs = chunks[ci]
        ps = slice(p * PAIR_W, (p + 1) * PAIR_W)
        tmb = tm.astype(BF16)
        vb_p = prep[seq]["vb"][rs, ps]
        kbg_p = prep[seq]["kbg"][rs, ps]
        u_list.append(vb_p + _dot(tmb, _row_stack_masked(vb_p.astype(BF16), grp_pw)))
        w_list.append(kbg_p + _dot(tmb, _row_stack_masked(kbg_p.astype(BF16), grp_pw)))
    unit_of = {u: i for i, u in enumerate(units)}

    outs = [[] for _ in range(nseq)]
    for ci, rs in enumerate(chunks):
        hook()
        for seq in range(nseq):
            pr = prep[seq]
            gx = pr["gc_x"][rs]
            glast = gx[CHUNK - 1:CHUNK, :]
            kd = pr["k"][rs] * jnp.exp(glast - gx)
            sdec = jnp.exp(glast)
            out_p = []
            for p in pairs:
                i = unit_of[(seq, ci, p)]
                wq = []
                for hh in range(2):
                    h = 2 * p + hh
                    th = slice(h * DN_HEAD_DIM, (h + 1) * DN_HEAD_DIM)
                    tp = slice(hh * DN_HEAD_DIM, (hh + 1) * DN_HEAD_DIM)
                    lhs = jnp.concatenate([w_list[i][:, tp], pr["qd"][rs, th]], axis=0).astype(BF16)
                    wq.append(_dot(lhs, st_ref[seq, h].astype(BF16)))
                vn = u_list[i] - jnp.concatenate([x[0:CHUNK] for x in wq], axis=1)
                vnb = vn.astype(BF16)
                ha, hb = 2 * p, 2 * p + 1
                kd_t = jnp.concatenate([kd[:, ha * DN_HEAD_DIM:(ha + 1) * DN_HEAD_DIM],
                                        kd[:, hb * DN_HEAD_DIM:(hb + 1) * DN_HEAD_DIM]], axis=0).T
                lhs = jnp.concatenate([attn_list[i], kd_t.astype(BF16)], axis=0)
                res = _dot(lhs, _row_stack_masked(vnb, grp_pw))
                out_p.append(jnp.concatenate([x[CHUNK:2 * CHUNK] for x in wq], axis=1) + res[0:CHUNK])
                for hh in range(2):
                    h = 2 * p + hh
                    th = slice(h * DN_HEAD_DIM, (h + 1) * DN_HEAD_DIM)
                    tp = slice(hh * DN_HEAD_DIM, (hh + 1) * DN_HEAD_DIM)
                    st_ref[seq, h] = st_ref[seq, h] * sdec[:, th] + res[CHUNK:CHUNK + DN_HEAD_DIM, tp]
            outs[seq].append(jnp.concatenate(out_p, axis=1))
    for seq in range(nseq):
        o = jnp.concatenate(outs[seq], axis=0)
        o = _head_norm(o, DN_HEAD_DIM, 1.0 / DN_HEAD_DIM) * ng_ref[...]
        o_ref[seq] = (o * _silu(gate_ref[seq].astype(F32))).astype(BF16)


def _gdn(main, small, nb, ns, cw, dtb, alog, ng, consts):
    nseq = _nseq(nb, GDN_NSEQ)
    spb = ns // ROWS
    main3 = main.reshape(nb, ns, MAIN_WIDTH)
    small3 = small.reshape(nb, ns, SMALL_WIDTH)
    rowmap = lambda blk: (lambda b, c: (b, c, blk))
    const = lambda shp: pl.BlockSpec(shp, lambda b, c: (0,) * len(shp))
    qw = DN_HEADS * CHUNK
    y = pl.pallas_call(
        functools.partial(_gdn_kernel, nseq=nseq),
        grid=(nb // nseq, spb),
        in_specs=[
            pl.BlockSpec((nseq, ROWS, DN_CONV_DIM), rowmap(DNQKV_BLK)),
            pl.BlockSpec((nseq, ROWS, DN_WIDTH), rowmap(DNGATE_BLK)),
            pl.BlockSpec((nseq, ROWS, SMALL_WIDTH), rowmap(0)),
            const((CONV_K, DN_CONV_DIM)),
            const((1, SMALL_WIDTH)), const((1, SMALL_WIDTH)), const((1, DN_WIDTH)),
            const((2 * SMALL_WIDTH, DN_WIDTH)), const((2 * SMALL_WIDTH, DN_WIDTH)),
            const((2 * SMALL_WIDTH, qw)),
            const((ROWS, ROWS)),
            const(((CONV_K - 1) * ROWS, ROWS)),
        ],
        out_specs=pl.BlockSpec((nseq, ROWS, DN_WIDTH), rowmap(0)),
        out_shape=jax.ShapeDtypeStruct((nb, ns, DN_WIDTH), BF16),
        scratch_shapes=[pltpu.VMEM((nseq, TAIL, DN_CONV_DIM), F32),
                        pltpu.VMEM((nseq, DN_HEADS, DN_HEAD_DIM, DN_HEAD_DIM), F32)],
        compiler_params=pltpu.CompilerParams(
            dimension_semantics=("arbitrary", "arbitrary"), vmem_limit_bytes=VMEM_LIMIT),
        name="gdn",
    )(main3, main3, small3, cw, dtb, alog, ng,
      consts["e_beta2"], consts["e_g2"], consts["e_gq2"], consts["ltri"],
      consts["shift"])
    return y.reshape(nb * ns, DN_WIDTH)


REC_NSEQ = 2
N_SSD_IN = 12
N_GDN_IN = 12


def _recurrent_kernel(*refs, nseq):
    ssd_in = refs[0:N_SSD_IN]
    gdn_in = refs[N_SSD_IN:N_SSD_IN + N_GDN_IN]
    y_ssd_ref, y_dn_ref, s_tail, s_st, g_tail, g_st = refs[N_SSD_IN + N_GDN_IN:]
    stages = iter(_ssd_stages(*ssd_in, y_ssd_ref, s_tail, s_st, nseq=nseq))

    def hook():
        stage = next(stages, None)
        if stage is not None:
            stage()

    _gdn_kernel(*gdn_in, y_dn_ref, g_tail, g_st, nseq=nseq, hook=hook)
    for stage in stages:
        stage()


def _recurrent(main, small, nb, ns, s_cw, s_cb, s_dtb, s_alog, s_dexp, s_ng, g_cw, g_dtb, g_alog, g_ng,
               consts):
    nseq = _nseq(nb, REC_NSEQ)
    spb = ns // ROWS
    main3 = main.reshape(nb, ns, MAIN_WIDTH)
    small3 = small.reshape(nb, ns, SMALL_WIDTH)
    rowmap = lambda blk: (lambda b, c: (b, c, blk))
    rows = lambda w, blk: pl.BlockSpec((nseq, ROWS, w), rowmap(blk))
    const = lambda shp: pl.BlockSpec(shp, lambda b, c: (0,) * len(shp))
    qw = DN_HEADS * CHUNK
    shift_spec = const(((CONV_K - 1) * ROWS, ROWS))
    y_ssd, y_dn = pl.pallas_call(
        functools.partial(_recurrent_kernel, nseq=nseq),
        grid=(nb // nseq, spb),
        in_specs=[
            rows(SSD_CONV_DIM, XBC_BLK), rows(SSD_WIDTH, Z_BLK), rows(SMALL_WIDTH, 0),
            const((CONV_K, SSD_CONV_DIM)), const((1, SSD_CONV_DIM)),
            const((1, SMALL_WIDTH)), const((1, SMALL_WIDTH)),
            const((1, SSD_WIDTH)), const((1, SSD_WIDTH)),
            const((2 * SMALL_WIDTH, SSD_WIDTH)), const((ROWS, ROWS)), shift_spec,
            rows(DN_CONV_DIM, DNQKV_BLK), rows(DN_WIDTH, DNGATE_BLK), rows(SMALL_WIDTH, 0),
            const((CONV_K, DN_CONV_DIM)),
            const((1, SMALL_WIDTH)), const((1, SMALL_WIDTH)), const((1, DN_WIDTH)),
            const((2 * SMALL_WIDTH, DN_WIDTH)), const((2 * SMALL_WIDTH, DN_WIDTH)),
            const((2 * SMALL_WIDTH, qw)), const((ROWS, ROWS)), shift_spec,
        ],
        out_specs=[rows(SSD_WIDTH, 0), rows(DN_WIDTH, 0)],
        out_shape=[jax.ShapeDtypeStruct((nb, ns, SSD_WIDTH), BF16),
                   jax.ShapeDtypeStruct((nb, ns, DN_WIDTH), BF16)],
        scratch_shapes=[pltpu.VMEM((nseq, TAIL, SSD_CONV_DIM), F32),
                        pltpu.VMEM((nseq, SSD_STATE, SSD_WIDTH), F32),
                        pltpu.VMEM((nseq, TAIL, DN_CONV_DIM), F32),
                        pltpu.VMEM((nseq, DN_HEADS, DN_HEAD_DIM, DN_HEAD_DIM), F32)],
        compiler_params=pltpu.CompilerParams(
            dimension_semantics=("arbitrary", "arbitrary"), vmem_limit_bytes=VMEM_LIMIT),
        name="recurrent",
    )(main3, main3, small3, s_cw, s_cb, s_dtb, s_alog, s_dexp, s_ng,
      consts["e_ssd2"], consts["ltri"], consts["shift"],
      main3, main3, small3, g_cw, g_dtb, g_alog, g_ng,
      consts["e_beta2"], consts["e_g2"], consts["e_gq2"], consts["ltri"], consts["shift"])
    return y_ssd.reshape(nb * ns, SSD_WIDTH), y_dn.reshape(nb * ns, DN_WIDTH)


_HEAD_PERM = np.concatenate([np.arange(0, SSD_HEADS, 2), np.arange(1, SSD_HEADS, 2)])


def _constants():
    r = np.arange(ROWS)
    same_chunk = (r[:, None] // CHUNK) == (r[None, :] // CHUNK)
    ltri = (same_chunk & (r[None, :] <= r[:, None])).astype(np.float32)
    e_ssd = np.zeros((SMALL_WIDTH, SSD_WIDTH), np.float32)
    for col, h in enumerate(_HEAD_PERM):
        e_ssd[col, h * SSD_HEAD_DIM:(h + 1) * SSD_HEAD_DIM] = 1.0
    qw = DN_HEADS * CHUNK
    e_beta = np.zeros((SMALL_WIDTH, DN_WIDTH), np.float32)
    e_g = np.zeros((SMALL_WIDTH, DN_WIDTH), np.float32)
    e_gq = np.zeros((SMALL_WIDTH, qw), np.float32)
    for h in range(DN_HEADS):
        e_beta[DNB_COL + h, h * DN_HEAD_DIM:(h + 1) * DN_HEAD_DIM] = 1.0
        e_g[DNA_COL + h, h * DN_HEAD_DIM:(h + 1) * DN_HEAD_DIM] = 1.0
        e_gq[DNA_COL + h, h * CHUNK:(h + 1) * CHUNK] = 1.0
    lane = np.arange(ATTN_WIDTH)
    bd64 = ((lane[:, None] // ATTN_HEAD_DIM) == (lane[None, :] // ATTN_HEAD_DIM)).astype(np.float32)
    bd64 = bd64 / ATTN_HEAD_DIM
    shift = np.zeros(((CONV_K - 1) * ROWS, ROWS), np.float32)
    for tap in range(CONV_K - 1):
        s = CONV_K - 1 - tap
        shift[tap * ROWS + np.arange(s, ROWS), np.arange(0, ROWS - s)] = 1.0
    twice = lambda m: np.concatenate([m, m], axis=0)
    out = dict(ltri=ltri, e_ssd2=twice(e_ssd), e_beta2=twice(e_beta), e_g2=twice(e_g),
               e_gq2=twice(e_gq), bd64=bd64, shift=shift)
    return {k: jnp.asarray(v, BF16) for k, v in out.items()}


def _attn_bias(rel_bias):
    r = np.arange(ATTN_TQ)[:, None]
    c = np.arange(ATTN_NK)[None, :]
    qc, kc = r // CHUNK, c // CHUNK
    valid = (kc >= qc) & (kc <= qc + ATTN_BAND_CHUNKS - 1)
    nh = rel_bias.shape[0]
    assert ATTN_TQ - 1 <= ATTN_REL_CLIP
    near = rel_bias[:, ATTN_REL_CLIP - (ATTN_TQ - 1):2 * ATTN_REL_CLIP]
    far = jnp.broadcast_to(rel_bias[:, 2 * ATTN_REL_CLIP:], (nh, ATTN_NK - ATTN_REL_CLIP))
    t = jnp.concatenate([near, far], axis=1)
    span = ATTN_TQ + ATTN_NK - 1
    w = jnp.concatenate([t[:, ::-1], jnp.zeros((nh, 1), rel_bias.dtype)], axis=1)
    skew = jnp.tile(w, (1, ATTN_TQ))[:, :ATTN_TQ * span].reshape(nh, ATTN_TQ, span)
    bias = skew[:, :, ATTN_TQ - 1:ATTN_TQ - 1 + ATTN_NK]
    bias = jnp.where(valid[None], bias, NEG).astype(F32)
    ext = jnp.concatenate([bias, jnp.full((nh, ATTN_TQ, ATTN_PAD), NEG, F32)], axis=2)
    offs = [max(ATTN_PAD - v * ATTN_TQ, 0) for v in range(ATTN_PAD // ATTN_TQ + 1)]
    return jnp.stack([ext[:, :, o:o + ATTN_NK] for o in offs], axis=0)


def _pad_row(v, start):
    return jnp.zeros((1, SMALL_WIDTH), F32).at[0, start:start + v.shape[0]].set(v.astype(F32))


def kernel(x, norm_g, w_in, ssd_conv_w, ssd_conv_b, ssd_dt_bias, ssd_a_log, ssd_d, ssd_norm_g,
           attn_q_norm_g, attn_k_norm_g, attn_rel_bias, dn_conv_w, dn_dt_bias, dn_a_log,
           dn_norm_g, w_out):
    nb, ns, _ = x.shape
    depth = w_in.shape[0]
    assert ns % ROWS == 0 and ns % ATTN_PAD == 0
    consts = _constants()
    x2 = x.reshape(nb * ns, D_MODEL)
    w_main = jnp.concatenate(
        [w_in[:, :, 1024:2560], w_in[:, :, 4624:6160], w_in[:, :, 0:1024], w_in[:, :, 2576:4624],
         w_in[:, :, 6160:6672]], axis=2).astype(BF16)
    w_small = jnp.concatenate(
        [w_in[:, :, 2560:2576][:, :, _HEAD_PERM], w_in[:, :, 6672:6680],
         jnp.zeros((depth, D_MODEL, SMALL_WIDTH - 24), F32)], axis=2).astype(BF16)
    wo = w_out.astype(BF16)
    pad_rows = lambda v, start: jnp.zeros((depth, 1, SMALL_WIDTH), F32).at[
        :, 0, start:start + v.shape[1]].set(v.astype(F32))
    ssd_dtb = pad_rows(ssd_dt_bias[:, _HEAD_PERM], 0)
    ssd_alog = pad_rows(ssd_a_log[:, _HEAD_PERM], 0)
    ssd_dexp = jnp.repeat(ssd_d.astype(F32), SSD_HEAD_DIM, axis=1)
    dn_dtb = pad_rows(dn_dt_bias, DNA_COL)
    dn_alog = pad_rows(dn_a_log, DNA_COL)
    gq = jnp.tile(attn_q_norm_g, (1, ATTN_HEADS))
    gk = jnp.tile(attn_k_norm_g, (1, ATTN_HEADS))
    dn_ng = jnp.tile(dn_norm_g, (1, DN_HEADS))
    bias = _attn_bias(attn_rel_bias.reshape(depth * ATTN_HEADS, -1))
    for l in range(depth):
        main, small, qn, kn, vn = _inproj(
            x2, norm_g[l][None, :], w_main, w_small, l, gq[l][None, :], gk[l][None, :], consts["bd64"])
        y_ssd, y_dn = _recurrent(
            main, small, nb, ns, ssd_conv_w[l], ssd_conv_b[l][None, :], ssd_dtb[l], ssd_alog[l],
            ssd_dexp[l][None, :], ssd_norm_g[l][None, :],
            dn_conv_w[l], dn_dtb[l], dn_alog[l], dn_ng[l][None, :], consts)
        y_attn = _attn(qn, kn, vn, main, nb, ns, bias, l)
        x2 = _outproj(x2, y_ssd, y_attn, y_dn, wo, l)
    return x2.reshape(nb, ns, D_MODEL)
```
